```python
import math
import jax, jax.numpy as jnp
from jax import lax
import numpy as np


D_MODEL = 1024
BATCH = 4
SEQ = 4096
DEPTH = 2

GRID_W = 64
CTX_LEN = 256
N_MIXERS = 2
EPS = 1e-6
D_RNN = 1024
RG_BLOCKS = 8
RG_BLOCK = D_RNN // RG_BLOCKS
CONV_W = 4
CONV_LEFT = 2
RG_C = 8.0
DA_HEADS = 8
DA_HEAD_DIM = D_MODEL // DA_HEADS // 2
ROPE_THETA = 10000.0
Q_BLOCK = 128
N_EXPERTS = 32
N_GROUPS = 4
EXPERTS_PER_GROUP = N_EXPERTS // N_GROUPS
TOP_K = 2
D_EXPERT = 512
MOE_BLOCK = 128

kernel_name = 'hybrid_rglru_diffattn_groupmoe_block'


def rms_norm(x, g):
    xf = x.astype(jnp.float32)
    y = xf * lax.rsqrt(jnp.mean(xf * xf, axis=-1, keepdims=True) + EPS)
    return y.astype(x.dtype) * g


def modulate(h, shift, scale):
    return h * (1.0 + scale) + shift


def adaln_chunks(cond, w_mod, b_mod):
    return jnp.split(jax.nn.silu(cond) @ w_mod + b_mod, 6, axis=-1)


def centred_dwconv(x, w, b):
    T = x.shape[1]
    xp = jnp.pad(x, ((0, 0), (CONV_LEFT, CONV_W - 1 - CONV_LEFT), (0, 0)))
    y = xp[:, 0:T] * w[0]
    for k in range(1, CONV_W):
        y = y + xp[:, k:k + T] * w[k]
    return y + b


def rglru_coeffs(u, wa, ba, wi, bi, lam):
    B_, T, _ = u.shape
    ur = u.reshape(B_, T, RG_BLOCKS, RG_BLOCK)
    r = jax.nn.sigmoid(jnp.einsum('btnk,nkj->btnj', ur, wa).reshape(B_, T, D_RNN) + ba)
    i = jax.nn.sigmoid(jnp.einsum('btnk,nkj->btnj', ur, wi).reshape(B_, T, D_RNN) + bi)
    log_a = -RG_C * r.astype(jnp.float32) * jax.nn.softplus(-lam.astype(jnp.float32))
    a = jnp.exp(log_a)
    b = jnp.sqrt(-jnp.expm1(2.0 * log_a)) * (i * u).astype(jnp.float32)
    return a, b


def _combine(left, right):
    a1, b1 = left
    a2, b2 = right
    return a1 * a2, a2 * b1 + b2


def linear_scan(a, b, h0, reverse):
    A, H = lax.associative_scan(_combine, (a, b), axis=1, reverse=reverse)
    if h0 is None:
        return H
    return H + A * h0[:, None, :]


def rglru_mixer(hx, hc, w_in, conv_w, conv_b, ga_w, ga_b, gx_w, gx_b, lam, w_out, need_ctx):
    def branches(h):
        y, u = jnp.split(h @ w_in, 2, axis=-1)
        return jax.nn.gelu(y), centred_dwconv(u, conv_w, conv_b)
    yx, ux = branches(hx)
    yc, uc = branches(hc)
    hs_x = None
    hs_c = None
    for d in range(2):
        rev = d == 1
        a_c, b_c = rglru_coeffs(uc, ga_w[d], ga_b[d], gx_w[d], gx_b[d], lam[d])
        h_c = linear_scan(a_c, b_c, None, rev)
        h0 = h_c[:, 0] if rev else h_c[:, -1]
        a_x, b_x = rglru_coeffs(ux, ga_w[d], ga_b[d], gx_w[d], gx_b[d], lam[d])
        h_x = linear_scan(a_x, b_x, h0, rev)
        hs_x = h_x if hs_x is None else hs_x + h_x
        hs_c = h_c if hs_c is None else hs_c + h_c
    out_x = (hs_x.astype(hx.dtype) * yx) @ w_out
    out_c = (hs_c.astype(hc.dtype) * yc) @ w_out if need_ctx else None
    return out_x, out_c


def rope_tables(n_tokens):
    n_rows = n_tokens // GRID_W
    row = jnp.repeat(jnp.arange(n_rows), GRID_W).astype(jnp.float32)
    col = jnp.tile(jnp.arange(GRID_W), n_rows).astype(jnp.float32)
    n_freq = DA_HEAD_DIM // 4
    inv = 1.0 / (ROPE_THETA ** (jnp.arange(n_freq, dtype=jnp.float32) / n_freq))
    ang = jnp.stack([row, col], axis=-1)[:, :, None] * inv
    ang = jnp.broadcast_to(ang[:, :, None, :], (n_tokens, 2, 2, n_freq)).reshape(n_tokens, DA_HEAD_DIM)
    return jnp.cos(ang), jnp.sin(ang)


def apply_rope(x, cos, sin):
    xr = x.reshape(x.shape[:-1] + (2, 2, DA_HEAD_DIM // 4))
    rot = jnp.concatenate([-xr[..., 1:2, :], xr[..., 0:1, :]], axis=-2).reshape(x.shape)
    c = cos[:, None, None, :].astype(x.dtype)
    s = sin[:, None, None, :].astype(x.dtype)
    return x * c + rot * s


def diff_softmax_attend(q, k, v, lam, subln_g, lambda_init):
    s = jnp.einsum('bhiqd,bhikd->bhiqk', q, k).astype(jnp.float32)
    p = jax.nn.softmax(s, axis=-1)
    w = p[:, :, 0] - lam * p[:, :, 1]
    o = jnp.einsum('bhqk,bhkv->bhqv', w.astype(v.dtype), v)
    return rms_norm(o, subln_g) * (1.0 - lambda_init)


def diff_attn_mixer(hx, hc, w_qkv, lam_vecs, subln_g, w_o, lambda_init, need_ctx):
    B_, S, _ = hx.shape
    C = hc.shape[1]

    def project(h):
        T = h.shape[1]
        q, k, v = jnp.split(h @ w_qkv, 3, axis=-1)
        return (q.reshape(B_, T, DA_HEADS, 2, DA_HEAD_DIM),
                k.reshape(B_, T, DA_HEADS, 2, DA_HEAD_DIM),
                v.reshape(B_, T, DA_HEADS, 2 * DA_HEAD_DIM))

    qx, kx, vx = project(hx)
    qc, kc, vc = project(hc)
    cos, sin = rope_tables(S)
    qx = apply_rope(qx, cos, sin)
    kx = apply_rope(kx, cos, sin)
    scale = DA_HEAD_DIM ** -0.5
    qk_heads = lambda t: jnp.transpose(t, (0, 2, 3, 1, 4))
    v_heads = lambda t: jnp.transpose(t, (0, 2, 1, 3))
    lf = lam_vecs.astype(jnp.float32)
    lam = jnp.exp(jnp.sum(lf[0] * lf[1])) - jnp.exp(jnp.sum(lf[2] * lf[3])) + lambda_init
    kc_h = qk_heads(kc)
    vc_h = v_heads(vc)
    k_all = jnp.concatenate([qk_heads(kx), kc_h], axis=3)
    v_all = jnp.concatenate([v_heads(vx), vc_h], axis=2)
    qx_h = qk_heads(qx) * scale
    nb = S // Q_BLOCK
    q_blocks = jnp.moveaxis(qx_h.reshape(B_, DA_HEADS, 2, nb, Q_BLOCK, DA_HEAD_DIM), 3, 0)
    o = lax.map(lambda qb: diff_softmax_attend(qb, k_all, v_all, lam, subln_g, lambda_init), q_blocks)
    o = jnp.transpose(o, (1, 0, 3, 2, 4)).reshape(B_, S, D_MODEL)
    out_x = o @ w_o
    out_c = None
    if need_ctx:
        oc = diff_softmax_attend(qk_heads(qc) * scale, kc_h, vc_h, lam, subln_g, lambda_init)
        out_c = jnp.transpose(oc, (0, 2, 1, 3)).reshape(B_, C, D_MODEL) @ w_o
    return out_x, out_c


def moe_ffn(h, router_w, router_bias, w_gate, w_up, w_down):
    N, D = h.shape
    s = jax.nn.sigmoid((h @ router_w).astype(jnp.float32))
    s_sel = s + router_bias.astype(jnp.float32)
    grp_score = lax.top_k(s_sel.reshape(N, N_GROUPS, EXPERTS_PER_GROUP), 2)[0].sum(-1)
    grp = jnp.argmax(grp_score, axis=-1)
    in_grp = (jnp.arange(N_EXPERTS) // EXPERTS_PER_GROUP)[None, :] == grp[:, None]
    _, idx = lax.top_k(jnp.where(in_grp, s_sel, -jnp.inf), TOP_K)
    wts = jnp.take_along_axis(s, idx, axis=-1)
    wts = wts / jnp.sum(wts, axis=-1, keepdims=True)
    e_flat = idx.reshape(-1)
    t_flat = jnp.repeat(jnp.arange(N), TOP_K)
    w_flat = wts.reshape(-1)
    order = jnp.argsort(e_flat)
    e_s, t_s, w_s = e_flat[order], t_flat[order], w_flat[order]
    counts = jnp.bincount(e_flat, length=N_EXPERTS)
    starts = jnp.cumsum(counts) - counts
    padded = ((counts + MOE_BLOCK - 1) // MOE_BLOCK) * MOE_BLOCK
    p_ends = jnp.cumsum(padded)
    p_starts = p_ends - padded
    dest = p_starts[e_s] + jnp.arange(N * TOP_K) - starts[e_s]
    n_blocks = -(-(N * TOP_K) // MOE_BLOCK) + N_EXPERTS
    P = n_blocks * MOE_BLOCK
    buf_tok = jnp.zeros((P,), jnp.int32).at[dest].set(t_s.astype(jnp.int32))
    buf_w = jnp.zeros((P,), jnp.float32).at[dest].set(w_s)
    blk_exp = jnp.minimum(jnp.searchsorted(p_ends, jnp.arange(n_blocks) * MOE_BLOCK, side='right'), N_EXPERTS - 1)
    xb = h[buf_tok].reshape(n_blocks, MOE_BLOCK, D)

    def expert_block(args):
        xblk, e = args
        return (jax.nn.silu(xblk @ w_gate[e]) * (xblk @ w_up[e])) @ w_down[e]

    yb = lax.map(expert_block, (xb, blk_exp)).reshape(P, D)
    return jnp.zeros_like(h).at[buf_tok].add(yb * buf_w[:, None].astype(yb.dtype))


def setup_inputs(seed: int = 0) -> dict:
    key = jax.random.key(seed)
    ks = iter(jax.random.split(key, 40))
    f32 = jnp.float32
    nrm = lambda shape, sc: jax.random.normal(next(ks), shape, f32) * sc
    n_rg = (DEPTH + N_MIXERS - 1) // N_MIXERS
    n_da = DEPTH // N_MIXERS
    D = D_MODEL
    a8 = jax.random.uniform(next(ks), (n_rg, 2, D_RNN), f32, 0.9, 0.999)
    p = a8 ** (1.0 / RG_C)
    rg_lambda = jnp.log(p) - jnp.log1p(-p)
    return {
        'x': nrm((BATCH, SEQ, D), 1.0),
        'c': nrm((BATCH, D), 1.0),
        'ctx': nrm((BATCH, CTX_LEN, D), 1.0),
        'c_ctx': nrm((D,), 1.0),
        'w_mod': nrm((DEPTH, D, 6 * D), 0.5 * D ** -0.5),
        'b_mod': nrm((DEPTH, 6 * D), 0.02),
        'norm1_g': 1.0 + nrm((DEPTH, D), 0.05),
        'norm2_g': 1.0 + nrm((DEPTH, D), 0.05),
        'rg_w_in': nrm((n_rg, D, 2 * D_RNN), D ** -0.5),
        'rg_conv_w': nrm((n_rg, CONV_W, D_RNN), CONV_W ** -0.5),
        'rg_conv_b': nrm((n_rg, D_RNN), 0.02),
        'rg_gate_a_w': nrm((n_rg, 2, RG_BLOCKS, RG_BLOCK, RG_BLOCK), RG_BLOCK ** -0.5),
        'rg_gate_a_b': nrm((n_rg, 2, D_RNN), 0.02),
        'rg_gate_x_w': nrm((n_rg, 2, RG_BLOCKS, RG_BLOCK, RG_BLOCK), RG_BLOCK ** -0.5),
        'rg_gate_x_b': nrm((n_rg, 2, D_RNN), 0.02),
        'rg_lambda': rg_lambda,
        'rg_w_out': nrm((n_rg, D_RNN, D), D_RNN ** -0.5),
        'da_w_qkv': nrm((n_da, D, 3 * D), D ** -0.5),
        'da_lambda': nrm((n_da, 4, DA_HEAD_DIM), 0.1),
        'da_subln_g': 1.0 + nrm((n_da, 2 * DA_HEAD_DIM), 0.05),
        'da_w_o': nrm((n_da, D, D), D ** -0.5),
        'router_w': nrm((D, N_EXPERTS), D ** -0.5),
        'router_bias': nrm((N_EXPERTS,), 0.01),
        'moe_w_gate': nrm((DEPTH, N_EXPERTS, D, D_EXPERT), D ** -0.5),
        'moe_w_up': nrm((DEPTH, N_EXPERTS, D, D_EXPERT), D ** -0.5),
        'moe_w_down': nrm((DEPTH, N_EXPERTS, D_EXPERT, D), D_EXPERT ** -0.5),
        'final_g': 1.0 + nrm((D,), 0.05),
    }


def reference(x, c, ctx, c_ctx, w_mod, b_mod, norm1_g, norm2_g, rg_w_in, rg_conv_w, rg_conv_b,
              rg_gate_a_w, rg_gate_a_b, rg_gate_x_w, rg_gate_x_b, rg_lambda, rg_w_out,
              da_w_qkv, da_lambda, da_subln_g, da_w_o, router_w, router_bias,
              moe_w_gate, moe_w_up, moe_w_down, final_g):
    B_, S, D = x.shape
    C = ctx.shape[1]
    cond_x = c[:, None, :]
    cond_c = c_ctx[None, None, :]
    i_rg = 0
    i_da = 0
    for layer in range(DEPTH):
        need_ctx = layer < DEPTH - 1
        shx1, scx1, gx1, shx2, scx2, gx2 = adaln_chunks(cond_x, w_mod[layer], b_mod[layer])
        shc1, scc1, gc1, shc2, scc2, gc2 = adaln_chunks(cond_c, w_mod[layer], b_mod[layer])
        hx = modulate(rms_norm(x, norm1_g[layer]), shx1, scx1)
        hc = modulate(rms_norm(ctx, norm1_g[layer]), shc1, scc1)
        if layer % N_MIXERS == 0:
            ox, oc = rglru_mixer(hx, hc, rg_w_in[i_rg], rg_conv_w[i_rg], rg_conv_b[i_rg],
                                 rg_gate_a_w[i_rg], rg_gate_a_b[i_rg], rg_gate_x_w[i_rg],
                                 rg_gate_x_b[i_rg], rg_lambda[i_rg], rg_w_out[i_rg], need_ctx)
            i_rg += 1
        else:
            lambda_init = 0.8 - 0.6 * math.exp(-0.3 * layer)
            ox, oc = diff_attn_mixer(hx, hc, da_w_qkv[i_da], da_lambda[i_da], da_subln_g[i_da],
                                     da_w_o[i_da], lambda_init, need_ctx)
            i_da += 1
        x = x + gx1 * ox
        hx2 = modulate(rms_norm(x, norm2_g[layer]), shx2, scx2)
        if need_ctx:
            ctx = ctx + gc1 * oc
            hc2 = modulate(rms_norm(ctx, norm2_g[layer]), shc2, scc2)
            tokens = jnp.concatenate([hx2.reshape(-1, D), hc2.reshape(-1, D)], axis=0)
            y = moe_ffn(tokens, router_w, router_bias, moe_w_gate[layer], moe_w_up[layer], moe_w_down[layer])
            x = x + gx2 * y[:B_ * S].reshape(B_, S, D)
            ctx = ctx + gc2 * y[B_ * S:].reshape(B_, C, D)
        else:
            y = moe_ffn(hx2.reshape(-1, D), router_w, router_bias, moe_w_gate[layer], moe_w_up[layer], moe_w_down[layer])
            x = x + gx2 * y.reshape(B_, S, D)
    return rms_norm(x, final_g)
```

```python
import functools
import math

import jax
import jax.numpy as jnp
from jax import lax
from jax.experimental import pallas as pl
from jax.experimental.pallas import tpu as pltpu

F32 = jnp.float32
BF16 = jnp.bfloat16

EPS = 1e-6
GRID_W = 64
CONV_W = 4
CONV_LEFT = 2
RG_C = 8.0
ROPE_THETA = 10000.0
N_GROUPS = 4
N_MIXERS = 2

LANES = 128
SUBLANES = 8
VMEM_LIMIT = 56 * 1024 * 1024

ROW_TILE = 512
MOE_TILE = 256
DMA_TILE = 256
ROUTE_TILE = 1024
Q_TILE = 256
KV_TILE = 512
SCAN_CHUNK = 256
COND_PAD = 8


def _cparams(n_axes):
    return pltpu.CompilerParams(dimension_semantics=("arbitrary",) * n_axes,
                                vmem_limit_bytes=VMEM_LIMIT)


def _adaln_kernel(cond_ref, w_ref, b_ref, o_ref):
    c = cond_ref[...]
    h = c * jax.nn.sigmoid(c)
    o_ref[0] = jnp.dot(h, w_ref[0], preferred_element_type=F32,
                       precision=lax.Precision.HIGHEST) + b_ref[0]


def _adaln(cond, w_mod, b_mod):
    depth, d, n6 = w_mod.shape
    tn = n6 // 6
    return pl.pallas_call(
        _adaln_kernel,
        grid=(depth, n6 // tn),
        in_specs=[pl.BlockSpec((COND_PAD, d), lambda l, j: (0, 0)),
                  pl.BlockSpec((1, d, tn), lambda l, j: (l, 0, j)),
                  pl.BlockSpec((1, 1, tn), lambda l, j: (l, 0, j))],
        out_specs=pl.BlockSpec((1, COND_PAD, tn), lambda l, j: (l, 0, j)),
        out_shape=jax.ShapeDtypeStruct((depth, COND_PAD, n6), F32),
        compiler_params=_cparams(2),
        name="adaln",
    )(cond, w_mod, b_mod.reshape(depth, 1, n6))


def _norm_mod(x, g, shift, scale):
    ms = jnp.mean(x * x, axis=-1, keepdims=True)
    h = (x * lax.rsqrt(ms + EPS)) * g
    return h * (1.0 + scale) + shift


class _Tok:
    def __init__(self, B, S, C, tile):
        assert S % tile == 0 and (B * C) % tile == 0
        self.B, self.S, self.C, self.tile = B, S, C, tile
        self.nx = B * S // tile
        self.nc = B * C // tile
        self.spt = S // tile

    def cond(self, i):
        return jnp.where(i < self.nx, i // self.spt, self.B)

    def mod_map(self, layer, chunk):
        return lambda i, *_: ((layer * COND_PAD + self.cond(i)) * 6 + chunk, 0, 0)


def _nm_matmul_kernel(x_ref, g_ref, sh_ref, sc_ref, w_ref, *rest, epilogue, out_scale):
    o_ref = rest[-1]
    h = _norm_mod(x_ref[...], g_ref[...], sh_ref[0], sc_ref[0])
    acc = jnp.dot(h.astype(BF16), w_ref[...], preferred_element_type=F32)
    if epilogue == "gelu":
        acc = jax.nn.gelu(acc, approximate=True)
        o_ref[...] = acc.astype(o_ref.dtype)
    elif epilogue == "rope":
        cos = rest[0][...]
        sin = rest[1][...]
        lane = lax.broadcasted_iota(jnp.int32, cos.shape, 1)
        first = (lane % 32) < 16
        for cb in range(acc.shape[1] // LANES):
            a = acc[:, cb * LANES:(cb + 1) * LANES]
            rot = jnp.where(first, pltpu.roll(a, LANES - 16, 1), pltpu.roll(a, 16, 1))
            r = (a * cos + rot * sin) * out_scale
            o_ref[:, cb * LANES:(cb + 1) * LANES] = r.astype(o_ref.dtype)
    else:
        o_ref[...] = acc.astype(o_ref.dtype)


def _nm_matmul(tok, layer, xall, g, modf, w, out_dtype, epilogue="none", tables=None, out_scale=1.0):
    ntok, d = xall.shape
    tm = tok.tile
    nw = w.shape[1]
    in_specs = [pl.BlockSpec((tm, d), lambda i: (i, 0)),
                pl.BlockSpec((1, d), lambda i: (0, 0)),
                pl.BlockSpec((1, 1, d), tok.mod_map(layer, 0)),
                pl.BlockSpec((1, 1, d), tok.mod_map(layer, 1)),
                pl.BlockSpec((d, nw), lambda i: (0, 0))]
    args = [xall, g, modf, modf, w]
    if epilogue == "rope":
        tmap = lambda i: (jnp.where(i < tok.nx, i % tok.spt, tok.spt), 0)
        in_specs += [pl.BlockSpec((tm, LANES), tmap), pl.BlockSpec((tm, LANES), tmap)]
        args += list(tables)
    return pl.pallas_call(
        functools.partial(_nm_matmul_kernel, epilogue=epilogue, out_scale=out_scale),
        grid=(ntok // tm,),
        in_specs=in_specs,
        out_specs=pl.BlockSpec((tm, nw), lambda i: (i, 0)),
        out_shape=jax.ShapeDtypeStruct((ntok, nw), out_dtype),
        compiler_params=_cparams(1),
        name="nm_matmul_" + epilogue,
    )(*args)


def _softplus(x):
    return jnp.maximum(x, 0.0) + jnp.log(1.0 + jnp.exp(-jnp.abs(x)))


def _rglru_kernel(ux_ref, uc_ref, yx_ref, yc_ref, cw_ref, cb_ref, wg_ref, gb_ref, lam_ref,
                  ox_ref, oc_ref, a_s, b_s, hf_s, hb_s, *, S, C, chunk):
    sp = _softplus(-lam_ref[...])
    cw = cw_ref[...]
    cb = cb_ref[...]
    wg = wg_ref[0]
    gb = gb_ref[0]

    def coeffs(u_ref, T):
        rc = min(chunk, T)

        def body(ci, _):
            c0 = pl.multiple_of(ci * rc, rc)
            cur = u_ref[pl.ds(c0, rc), :]
            pstart = pl.multiple_of(jnp.maximum(c0 - SUBLANES, 0), SUBLANES)
            nstart = pl.multiple_of(jnp.minimum(c0 + rc, T - SUBLANES), SUBLANES)
            prev = jnp.where(c0 > 0, u_ref[pl.ds(pstart, SUBLANES), :], 0.0)
            nxt = jnp.where(c0 + rc < T, u_ref[pl.ds(nstart, SUBLANES), :], 0.0)
            ext = jnp.concatenate([prev, cur, nxt], axis=0)
            n_ext = rc + 2 * SUBLANES
            xm2 = pltpu.roll(ext, 2, 0)[SUBLANES:SUBLANES + rc]
            xm1 = pltpu.roll(ext, 1, 0)[SUBLANES:SUBLANES + rc]
            xp1 = pltpu.roll(ext, n_ext - 1, 0)[SUBLANES:SUBLANES + rc]
            u = xm2 * cw[0:1] + xm1 * cw[1:2] + cur * cw[2:3] + xp1 * cw[3:4] + cb
            z = jnp.dot(u.astype(BF16), wg, preferred_element_type=F32) + gb
            for d in range(2):
                r = jax.nn.sigmoid(z[:, d * 2 * LANES:d * 2 * LANES + LANES])
                ig = jax.nn.sigmoid(z[:, d * 2 * LANES + LANES:(d + 1) * 2 * LANES])
                log_a = (-RG_C * r) * sp[d:d + 1]
                a = jnp.exp(log_a)
                bcoef = jnp.sqrt(1.0 - jnp.exp(2.0 * log_a)) * (ig * u)
                a_s[d, pl.ds(c0, rc), :] = a
                b_s[d, pl.ds(c0, rc), :] = bcoef
            return 0

        lax.fori_loop(0, T // rc, body, 0)

    row = lax.broadcasted_iota(jnp.int32, (SUBLANES, LANES), 0)

    def scans(T, cf, cbk):
        def body(k, carry):
            cf, cbk = carry
            rf = pl.multiple_of(k * SUBLANES, SUBLANES)
            rb = pl.multiple_of(T - SUBLANES - k * SUBLANES, SUBLANES)
            a = a_s[0, pl.ds(rf, SUBLANES), :]
            b = b_s[0, pl.ds(rf, SUBLANES), :]
            for s in (1, 2, 4):
                m = row >= s
                b = jnp.where(m, a * pltpu.roll(b, s, 0), 0.0) + b
                a = jnp.where(m, a * pltpu.roll(a, s, 0), a)
            h = b + a * cf
            hf_s[pl.ds(rf, SUBLANES), :] = h
            cf = h[SUBLANES - 1:SUBLANES, :]
            a = a_s[1, pl.ds(rb, SUBLANES), :]
            b = b_s[1, pl.ds(rb, SUBLANES), :]
            for s in (1, 2, 4):
                m = row < SUBLANES - s
                b = jnp.where(m, a * pltpu.roll(b, SUBLANES - s, 0), 0.0) + b
                a = jnp.where(m, a * pltpu.roll(a, SUBLANES - s, 0), a)
            h = b + a * cbk
            hb_s[pl.ds(rb, SUBLANES), :] = h
            cbk = h[0:1, :]
            return cf, cbk

        return lax.fori_loop(0, T // SUBLANES, body, (cf, cbk))

    zero = jnp.zeros((1, LANES), F32)
    coeffs(uc_ref, C)
    scans(C, zero, zero)
    oc_ref[...] = ((hf_s[0:C, :] + hb_s[0:C, :]) * yc_ref[...].astype(F32)).astype(oc_ref.dtype)
    h0f = hf_s[C - 1:C, :]
    h0b = hb_s[0:1, :]
    coeffs(ux_ref, S)
    scans(S, h0f, h0b)
    ox_ref[...] = ((hf_s[0:S, :] + hb_s[0:S, :]) * yx_ref[...].astype(F32)).astype(ox_ref.dtype)


def _rglru(B, S, C, u_all, y_all, conv_w, conv_b, wg, gb, lam):
    d_rnn = u_all.shape[1]
    nblk = d_rnn // LANES
    cbase = B * S // C
    tmax = max(S, C)
    return pl.pallas_call(
        functools.partial(_rglru_kernel, S=S, C=C, chunk=SCAN_CHUNK),
        grid=(B, nblk),
        in_specs=[pl.BlockSpec((S, LANES), lambda b, n: (b, n)),
                  pl.BlockSpec((C, LANES), lambda b, n: (cbase + b, n)),
                  pl.BlockSpec((S, LANES), lambda b, n: (b, n)),
                  pl.BlockSpec((C, LANES), lambda b, n: (cbase + b, n)),
                  pl.BlockSpec((CONV_W, LANES), lambda b, n: (0, n)),
                  pl.BlockSpec((1, LANES), lambda b, n: (0, n)),
                  pl.BlockSpec((1, LANES, 4 * LANES), lambda b, n: (n, 0, 0)),
                  pl.BlockSpec((1, 1, 4 * LANES), lambda b, n: (n, 0, 0)),
                  pl.BlockSpec((2, LANES), lambda b, n: (0, n))],
        out_specs=[pl.BlockSpec((S, LANES), lambda b, n: (b, n)),
                   pl.BlockSpec((C, LANES), lambda b, n: (b, n))],
        out_shape=[jax.ShapeDtypeStruct((B * S, d_rnn), BF16),
                   jax.ShapeDtypeStruct((B * C, d_rnn), BF16)],
        scratch_shapes=[pltpu.VMEM((2, tmax, LANES), F32), pltpu.VMEM((2, tmax, LANES), F32),
                        pltpu.VMEM((tmax, LANES), F32), pltpu.VMEM((tmax, LANES), F32)],
        compiler_params=_cparams(2),
        name="rglru",
    )(u_all, u_all, y_all, y_all, conv_w, conv_b, wg, gb, lam)


def _attn_kernel(q_ref, kx_ref, kc_ref, vx_ref, vc_ref, lam_ref, g_ref, o_ref, *, S, C, tk, lambda_init):
    q = q_ref[...]
    tq, dh2 = q.shape
    half = dh2 // 2
    lane = lax.broadcasted_iota(jnp.int32, q.shape, 1)
    zero = jnp.zeros_like(q)
    qs = (jnp.where(lane < half, q, zero), jnp.where(lane >= half, q, zero))

    def update(st, kblk, vblk):
        new = []
        for mi in range(2):
            m, l, acc = st[mi]
            s = lax.dot_general(qs[mi], kblk, (((1,), (1,)), ((), ())), preferred_element_type=F32)
            mn = jnp.maximum(m, jnp.max(s, axis=-1, keepdims=True))
            alpha = jnp.exp(m - mn)
            p = jnp.exp(s - mn)
            l = alpha * l + jnp.sum(p, axis=-1, keepdims=True)
            acc = alpha * acc + jnp.dot(p.astype(BF16), vblk, preferred_element_type=F32)
            new.append((mn, l, acc))
        return tuple(new)

    init = tuple((jnp.full((tq, 1), -jnp.inf, F32), jnp.zeros((tq, 1), F32), jnp.zeros((tq, dh2), F32))
                 for _ in range(2))

    def body(j, st):
        r0 = pl.multiple_of(j * tk, tk)
        return update(st, kx_ref[pl.ds(r0, tk), :], vx_ref[pl.ds(r0, tk), :])

    st = lax.fori_loop(0, S // tk, body, init)
    st = update(st, kc_ref[...], vc_ref[...])
    lv = lam_ref[...]
    lam = (jnp.exp(jnp.sum(lv[0:1] * lv[1:2], axis=-1, keepdims=True))
           - jnp.exp(jnp.sum(lv[2:3] * lv[3:4], axis=-1, keepdims=True)) + lambda_init)
    o = st[0][2] / st[0][1] - lam * (st[1][2] / st[1][1])
    ms = jnp.mean(o * o, axis=-1, keepdims=True)
    o = (o * lax.rsqrt(ms + EPS)) * g_ref[...] * (1.0 - lambda_init)
    o_ref[...] = o.astype(o_ref.dtype)


def _attention(B, S, C, heads, q_all, k_all, v_all, lam, subln_g, lambda_init):
    d = q_all.shape[1]
    dh2 = d // heads
    tq = min(Q_TILE, S)
    tk = min(KV_TILE, S)
    qpb = S // tq
    cbase = B * S // C
    return pl.pallas_call(
        functools.partial(_attn_kernel, S=S, C=C, tk=tk, lambda_init=lambda_init),
        grid=(B, heads, qpb),
        in_specs=[pl.BlockSpec((tq, dh2), lambda b, h, i: (b * qpb + i, h)),
                  pl.BlockSpec((S, dh2), lambda b, h, i: (b, h)),
                  pl.BlockSpec((C, dh2), lambda b, h, i: (cbase + b, h)),
                  pl.BlockSpec((S, dh2), lambda b, h, i: (b, h)),
                  pl.BlockSpec((C, dh2), lambda b, h, i: (cbase + b, h)),
                  pl.BlockSpec(lam.shape, lambda b, h, i: (0, 0)),
                  pl.BlockSpec((1, dh2), lambda b, h, i: (0, 0))],
        out_specs=pl.BlockSpec((tq, dh2), lambda b, h, i: (b * qpb + i, h)),
        out_shape=jax.ShapeDtypeStruct((B * S, d), BF16),
        compiler_params=_cparams(3),
        name="diff_attn",
    )(q_all, k_all, k_all, v_all, v_all, lam, subln_g)


def _dot_nt(a, b):
    return lax.dot_general(a, b, (((1,), (1,)), ((), ())), preferred_element_type=F32)


def _proj_router_kernel(*refs, two_sources, nx):
    if two_sources:
        gx_ref, gc_ref = refs[:2]
        refs = refs[2:]
    else:
        gx_ref = refs[0]
        refs = refs[1:]
    (w_ref, x_ref, gate_ref, g2_ref, sh_ref, sc_ref, rwh_ref, rwl_ref, x1_ref, h2_ref, st_ref) = refs
    if two_sources:
        i = pl.program_id(0)
        gin = jnp.where(i < nx, gx_ref[...], gc_ref[...])
    else:
        gin = gx_ref[...]
    o = jnp.dot(gin, w_ref[...], preferred_element_type=F32)
    x1 = x_ref[...] + gate_ref[0] * o
    x1_ref[...] = x1
    h2 = _norm_mod(x1, g2_ref[...], sh_ref[0], sc_ref[0])
    h2_ref[...] = h2
    hi = h2.astype(BF16)
    lo = (h2 - hi.astype(F32)).astype(BF16)
    rwh = rwh_ref[...]
    logits = _dot_nt(rwh, hi) + _dot_nt(rwh, lo) + _dot_nt(rwl_ref[...], hi)
    st_ref[...] = jax.nn.sigmoid(logits)


def _proj_router(tok, layer, n_rows, g_x, g_c, w, xall, norm2_g, modf, rw_hi, rw_lo):
    d = xall.shape[1]
    tm = tok.tile
    e = rw_hi.shape[0]
    two = g_c is not None
    nxm1 = tok.nx - 1
    in_specs = []
    args = []
    if two:
        in_specs += [pl.BlockSpec((tm, d), lambda i: (jnp.minimum(i, nxm1), 0)),
                     pl.BlockSpec((tm, d), lambda i: (jnp.maximum(i - tok.nx, 0), 0))]
        args += [g_x, g_c]
    else:
        in_specs += [pl.BlockSpec((tm, d), lambda i: (i, 0))]
        args += [g_x]
    in_specs += [pl.BlockSpec((d, d), lambda i: (0, 0)),
                 pl.BlockSpec((tm, d), lambda i: (i, 0)),
                 pl.BlockSpec((1, 1, d), tok.mod_map(layer, 2)),
                 pl.BlockSpec((1, d), lambda i: (0, 0)),
                 pl.BlockSpec((1, 1, d), tok.mod_map(layer, 3)),
                 pl.BlockSpec((1, 1, d), tok.mod_map(layer, 4)),
                 pl.BlockSpec((e, d), lambda i: (0, 0)),
                 pl.BlockSpec((e, d), lambda i: (0, 0))]
    args += [w, xall, modf, norm2_g, modf, modf, rw_hi, rw_lo]
    return pl.pallas_call(
        functools.partial(_proj_router_kernel, two_sources=two, nx=tok.nx),
        grid=(n_rows // tm,),
        in_specs=in_specs,
        out_specs=[pl.BlockSpec((tm, d), lambda i: (i, 0)),
                   pl.BlockSpec((tm, d), lambda i: (i, 0)),
                   pl.BlockSpec((e, tm), lambda i: (0, i))],
        out_shape=[jax.ShapeDtypeStruct((n_rows, d), F32),
                   jax.ShapeDtypeStruct((n_rows, d), F32),
                   jax.ShapeDtypeStruct((e, n_rows), F32)],
        compiler_params=_cparams(1),
        name="proj_router",
    )(*args)


def _route_kernel(s_ref, bias_ref, u_ref, ints_ref, wts_ref, cnt_ref, carry_ref, *, n_groups):
    i = pl.program_id(0)

    @pl.when(i == 0)
    def _():
        carry_ref[...] = jnp.zeros_like(carry_ref)

    s = s_ref[...]
    n_e, tr = s.shape
    epg = n_e // n_groups
    sel = s + bias_ref[...]
    ridx = lax.broadcasted_iota(jnp.int32, (epg, tr), 0).astype(F32)
    big = float(epg)
    best = gi = bi1 = bi2 = None
    for g in range(n_groups):
        sg = sel[g * epg:(g + 1) * epg, :]
        m1 = jnp.max(sg, axis=0, keepdims=True)
        i1 = jnp.min(jnp.where(sg == m1, ridx, big), axis=0, keepdims=True)
        sg2 = jnp.where(ridx == i1, -jnp.inf, sg)
        m2 = jnp.max(sg2, axis=0, keepdims=True)
        i2 = jnp.min(jnp.where(sg2 == m2, ridx, big), axis=0, keepdims=True)
        score = m1 + m2
        if g == 0:
            best, gi, bi1, bi2 = score, jnp.zeros_like(i1), i1, i2
        else:
            better = score > best
            best = jnp.where(better, score, best)
            gi = jnp.where(better, float(g), gi)
            bi1 = jnp.where(better, i1, bi1)
            bi2 = jnp.where(better, i2, bi2)
    e0 = gi * epg + bi1
    e1 = gi * epg + bi2
    eidx = lax.broadcasted_iota(jnp.int32, (n_e, tr), 0).astype(F32)
    oh0 = eidx == e0
    oh1 = eidx == e1
    w0 = jnp.sum(jnp.where(oh0, s, 0.0), axis=0, keepdims=True)
    w1 = jnp.sum(jnp.where(oh1, s, 0.0), axis=0, keepdims=True)
    den = w0 + w1
    wts_ref[0:1, :] = w0 / den
    wts_ref[1:2, :] = w1 / den
    f0 = jnp.where(oh0, 1.0, 0.0)
    f1 = jnp.where(oh1, 1.0, 0.0)
    u = u_ref[...]
    p0 = jnp.dot(f0.astype(BF16), u, preferred_element_type=F32)
    p1 = jnp.dot(f1.astype(BF16), u, preferred_element_type=F32)
    carry = carry_ref[...]
    c0 = jnp.sum(f0, axis=1, keepdims=True)
    c1 = jnp.sum(f1, axis=1, keepdims=True)
    rank0 = jnp.sum(f0 * (carry + p0), axis=0, keepdims=True)
    rank1 = jnp.sum(f1 * (carry + c0 + p1), axis=0, keepdims=True)
    ints_ref[0:1, :] = e0.astype(jnp.int32)
    ints_ref[1:2, :] = e1.astype(jnp.int32)
    ints_ref[2:3, :] = rank0.astype(jnp.int32)
    ints_ref[3:4, :] = rank1.astype(jnp.int32)
    carry = carry + c0 + c1
    carry_ref[...] = carry
    cnt_ref[...] = jnp.broadcast_to(carry, cnt_ref.shape)


def _route(scores_t, bias):
    n_e, nt = scores_t.shape
    tr = max(t for t in range(LANES, min(ROUTE_TILE, nt) + 1, LANES) if nt % t == 0)
    tri = jnp.triu(jnp.ones((tr, tr), BF16), k=1)
    return pl.pallas_call(
        functools.partial(_route_kernel, n_groups=N_GROUPS),
        grid=(nt // tr,),
        in_specs=[pl.BlockSpec((n_e, tr), lambda i: (0, i)),
                  pl.BlockSpec((n_e, 1), lambda i: (0, 0)),
                  pl.BlockSpec((tr, tr), lambda i: (0, 0))],
        out_specs=[pl.BlockSpec((4, tr), lambda i: (0, i)),
                   pl.BlockSpec((2, tr), lambda i: (0, i)),
                   pl.BlockSpec((n_e, LANES), lambda i: (0, 0))],
        out_shape=[jax.ShapeDtypeStruct((4, nt), jnp.int32),
                   jax.ShapeDtypeStruct((2, nt), F32),
                   jax.ShapeDtypeStruct((n_e, LANES), F32)],
        scratch_shapes=[pltpu.VMEM((n_e, 1), F32)],
        compiler_params=_cparams(1),
        name="route",
    )(scores_t, bias.reshape(n_e, 1), tri)


def _dispatch_kernel(pstart_ref, ints_ref, h_ref, xs_in_ref, xs_ref, sem, *, td):
    del xs_in_ref

    def start(t, _):
        for k in range(2):
            dst = pstart_ref[ints_ref[k, t]] + ints_ref[2 + k, t]
            pltpu.make_async_copy(h_ref.at[pl.ds(t, 1), :], xs_ref.at[pl.ds(dst, 1), :], sem).start()
        return 0

    lax.fori_loop(0, td, start, 0)

    def wait(t, _):
        for k in range(2):
            pltpu.make_async_copy(h_ref.at[pl.ds(0, 1), :], xs_ref.at[pl.ds(0, 1), :], sem).wait()
        return 0

    lax.fori_loop(0, td, wait, 0)


def _dispatch(pstart, ints, h2, n_slots):
    nt, d = h2.shape
    td = min(DMA_TILE, nt)
    xs0 = jnp.zeros((n_slots, d), h2.dtype)
    gs = pltpu.PrefetchScalarGridSpec(
        num_scalar_prefetch=1,
        grid=(nt // td,),
        in_specs=[pl.BlockSpec((4, td), lambda i, p: (0, i), memory_space=pltpu.SMEM),
                  pl.BlockSpec((td, d), lambda i, p: (i, 0)),
                  pl.BlockSpec(memory_space=pl.ANY)],
        out_specs=pl.BlockSpec(memory_space=pl.ANY),
        scratch_shapes=[pltpu.SemaphoreType.DMA(())],
    )
    return pl.pallas_call(
        functools.partial(_dispatch_kernel, td=td),
        grid_spec=gs,
        out_shape=jax.ShapeDtypeStruct((n_slots, d), h2.dtype),
        input_output_aliases={3: 0},
        compiler_params=_cparams(1),
        name="moe_dispatch",
    )(pstart, ints, h2, xs0)


def _ffn_kernel(be_ref, bi_ref, nu_ref, xs_ref, wg_ref, wu_ref, wd_ref, y_ref, wgb, wub, wdb):
    j = pl.program_id(0)

    @pl.when(j < nu_ref[0])
    def _():
        prev = be_ref[jnp.maximum(j - 1, 0)]
        changed = jnp.logical_or(j == 0, be_ref[j] != prev)

        @pl.when(changed)
        def _():
            wgb[...] = wg_ref[0].astype(BF16)
            wub[...] = wu_ref[0].astype(BF16)
            wdb[...] = wd_ref[0].astype(BF16)

        x = xs_ref[...].astype(BF16)
        g = jnp.dot(x, wgb[...], preferred_element_type=F32)
        u = jnp.dot(x, wub[...], preferred_element_type=F32)
        h = (g * jax.nn.sigmoid(g)) * u
        y_ref[...] = jnp.dot(h.astype(BF16), wdb[...], preferred_element_type=F32)

    @pl.when(j >= nu_ref[0])
    def _():
        y_ref[...] = jnp.zeros_like(y_ref)


def _ffn(blk_e, blk_in, n_used, xs, w_gate, w_up, w_down):
    n_slots, d = xs.shape
    n_e, _, de = w_gate.shape
    tb = MOE_TILE
    nb = n_slots // tb
    gs = pltpu.PrefetchScalarGridSpec(
        num_scalar_prefetch=3,
        grid=(nb,),
        in_specs=[pl.BlockSpec((tb, d), lambda j, be, bi, nu: (bi[j], 0)),
                  pl.BlockSpec((1, d, de), lambda j, be, bi, nu: (be[j], 0, 0)),
                  pl.BlockSpec((1, d, de), lambda j, be, bi, nu: (be[j], 0, 0)),
                  pl.BlockSpec((1, de, d), lambda j, be, bi, nu: (be[j], 0, 0))],
        out_specs=pl.BlockSpec((tb, d), lambda j, be, bi, nu: (j, 0)),
        scratch_shapes=[pltpu.VMEM((d, de), BF16), pltpu.VMEM((d, de), BF16), pltpu.VMEM((de, d), BF16)],
    )
    return pl.pallas_call(
        _ffn_kernel,
        grid_spec=gs,
        out_shape=jax.ShapeDtypeStruct((n_slots, d), F32),
        compiler_params=_cparams(1),
        name="moe_ffn",
    )(blk_e, blk_in, n_used, xs, w_gate, w_up, w_down)


def _combine_kernel(pstart_ref, ints_ref, wts_ref, x1_ref, gate_ref, *rest, td, final):
    if final:
        fg_ref, yb_ref, o_ref, ybuf, sem = rest
    else:
        yb_ref, o_ref, ybuf, sem = rest

    def start(t, _):
        for k in range(2):
            src = pstart_ref[ints_ref[k, t]] + ints_ref[2 + k, t]
            pltpu.make_async_copy(yb_ref.at[pl.ds(src, 1), :], ybuf.at[k, pl.ds(t, 1), :], sem).start()
        return 0

    lax.fori_loop(0, td, start, 0)

    def wait(t, _):
        for k in range(2):
            pltpu.make_async_copy(yb_ref.at[pl.ds(0, 1), :], ybuf.at[k, pl.ds(0, 1), :], sem).wait()
        return 0

    lax.fori_loop(0, td, wait, 0)
    w = wts_ref[...]
    y = ybuf[0] * w[:, 0:1] + ybuf[1] * w[:, 1:2]
    xn = x1_ref[...] + gate_ref[0] * y
    if final:
        ms = jnp.mean(xn * xn, axis=-1, keepdims=True)
        xn = (xn * lax.rsqrt(ms + EPS)) * fg_ref[...]
    o_ref[...] = xn


def _combine(tok, layer, pstart, ints, wts_rows, x1, modf, yb, final_g):
    nt, d = x1.shape
    td = tok.tile
    final = final_g is not None
    in_specs = [pl.BlockSpec((4, td), lambda i, p: (0, i), memory_space=pltpu.SMEM),
                pl.BlockSpec((td, 2), lambda i, p: (i, 0)),
                pl.BlockSpec((td, d), lambda i, p: (i, 0)),
                pl.BlockSpec((1, 1, d), tok.mod_map(layer, 5))]
    args = [ints, wts_rows, x1, modf]
    if final:
        in_specs.append(pl.BlockSpec((1, d), lambda i, p: (0, 0)))
        args.append(final_g)
    in_specs.append(pl.BlockSpec(memory_space=pl.ANY))
    args.append(yb)
    gs = pltpu.PrefetchScalarGridSpec(
        num_scalar_prefetch=1,
        grid=(nt // td,),
        in_specs=in_specs,
        out_specs=pl.BlockSpec((td, d), lambda i, p: (i, 0)),
        scratch_shapes=[pltpu.VMEM((2, td, d), F32), pltpu.SemaphoreType.DMA(())],
    )
    return pl.pallas_call(
        functools.partial(_combine_kernel, td=td, final=final),
        grid_spec=gs,
        out_shape=jax.ShapeDtypeStruct((nt, d), F32),
        compiler_params=_cparams(1),
        name="moe_combine",
    )(pstart, *args)


def _moe(tok_dma, layer, scores_t, h2, x1, modf, router_bias, w_gate, w_up, w_down, final_g):
    nt = h2.shape[0]
    n_e = w_gate.shape[0]
    tb = MOE_TILE
    ints, wts, cnt = _route(scores_t, router_bias.astype(F32))
    counts = cnt[:, 0].astype(jnp.int32)
    padded = ((counts + tb - 1) // tb) * tb
    p_end = jnp.cumsum(padded)
    p_start = (p_end - padded).astype(jnp.int32)
    nb = (2 * nt) // tb + n_e
    n_used = (p_end[-1] // tb).astype(jnp.int32)
    jb = jnp.arange(nb, dtype=jnp.int32)
    be = jnp.minimum(jnp.searchsorted(p_end, jb * tb, side="right"), n_e - 1).astype(jnp.int32)
    last = jnp.maximum(n_used - 1, 0)
    blk_e = jnp.where(jb < n_used, be, be[last])
    blk_in = jnp.minimum(jb, last)
    xs = _dispatch(p_start, ints, h2, nb * tb)
    yb = _ffn(blk_e, blk_in, n_used.reshape(1), xs, w_gate, w_up, w_down)
    return _combine(tok_dma, layer, p_start, ints, wts.T, x1, modf, yb, final_g)


def _rope_tables(S, tile):
    n_rows = S // GRID_W
    row = jnp.repeat(jnp.arange(n_rows), GRID_W).astype(F32)
    col = jnp.tile(jnp.arange(GRID_W), n_rows).astype(F32)
    n_freq = 16
    inv = 1.0 / (ROPE_THETA ** (jnp.arange(n_freq, dtype=F32) / n_freq))
    ang = jnp.stack([row, col], axis=-1)[:, :, None] * inv
    ang = jnp.broadcast_to(ang[:, :, None, :], (S, 2, 2, n_freq)).reshape(S, 4 * n_freq)
    cos, sin = jnp.cos(ang), jnp.sin(ang)
    sign = jnp.where((jnp.arange(4 * n_freq) % 32) < 16, -1.0, 1.0).astype(F32)
    sin = sin * sign
    cos = jnp.concatenate([jnp.tile(cos, (1, 2)), jnp.ones((tile, LANES), F32)], axis=0)
    sin = jnp.concatenate([jnp.tile(sin, (1, 2)), jnp.zeros((tile, LANES), F32)], axis=0)
    return cos, sin


def kernel(x, c, ctx, c_ctx, w_mod, b_mod, norm1_g, norm2_g, rg_w_in, rg_conv_w, rg_conv_b,
           rg_gate_a_w, rg_gate_a_b, rg_gate_x_w, rg_gate_x_b, rg_lambda, rg_w_out,
           da_w_qkv, da_lambda, da_subln_g, da_w_o, router_w, router_bias,
           moe_w_gate, moe_w_up, moe_w_down, final_g):
    B, S, D = x.shape
    C = ctx.shape[1]
    depth = w_mod.shape[0]
    assert B + 1 <= COND_PAD and D % LANES == 0
    assert all(l % N_MIXERS == 0 or l == depth - 1 for l in range(depth))
    tok = _Tok(B, S, C, min(ROW_TILE, S, B * C))
    tok_dma = _Tok(B, S, C, min(DMA_TILE, S, B * C))

    cond = jnp.zeros((COND_PAD, D), F32).at[:B].set(c).at[B].set(c_ctx)
    mods = _adaln(cond, w_mod, b_mod)
    modf = mods.reshape(depth * COND_PAD * 6, 1, D)
    xall = jnp.concatenate([x.reshape(B * S, D), ctx.reshape(B * C, D)], axis=0)
    rw_t = router_w.T.astype(F32)
    rw_hi = rw_t.astype(BF16)
    rw_lo = (rw_t - rw_hi.astype(F32)).astype(BF16)

    i_rg = 0
    i_da = 0
    out = None
    for layer in range(depth):
        last = layer == depth - 1
        n1 = norm1_g[layer].reshape(1, D)
        n2 = norm2_g[layer].reshape(1, D)
        if layer % N_MIXERS == 0:
            d_rnn = rg_w_in.shape[2] // 2
            w_in = rg_w_in[i_rg].astype(BF16)
            y_all = _nm_matmul(tok, layer, xall, n1, modf, w_in[:, :d_rnn], BF16, epilogue="gelu")
            u_all = _nm_matmul(tok, layer, xall, n1, modf, w_in[:, d_rnn:], F32)
            nblk = rg_gate_a_w.shape[2]
            wg = jnp.concatenate([rg_gate_a_w[i_rg, 0], rg_gate_x_w[i_rg, 0],
                                  rg_gate_a_w[i_rg, 1], rg_gate_x_w[i_rg, 1]], axis=-1).astype(BF16)
            bsplit = lambda v: v.reshape(nblk, 1, LANES)
            gb = jnp.concatenate([bsplit(rg_gate_a_b[i_rg, 0]), bsplit(rg_gate_x_b[i_rg, 0]),
                                  bsplit(rg_gate_a_b[i_rg, 1]), bsplit(rg_gate_x_b[i_rg, 1])], axis=-1)
            g_x, g_c = _rglru(B, S, C, u_all, y_all, rg_conv_w[i_rg], rg_conv_b[i_rg].reshape(1, d_rnn),
                              wg, gb, rg_lambda[i_rg])
            w_o = rg_w_out[i_rg].astype(BF16)
            i_rg += 1
        else:
            dh = da_lambda.shape[2]
            heads = D // (2 * dh)
            lambda_init = 0.8 - 0.6 * math.exp(-0.3 * layer)
            wqkv = da_w_qkv[i_da].astype(BF16)
            tables = _rope_tables(S, tok.tile)
            q_all = _nm_matmul(tok, layer, xall, n1, modf, wqkv[:, :D], BF16, epilogue="rope",
                               tables=tables, out_scale=dh ** -0.5)
            k_all = _nm_matmul(tok, layer, xall, n1, modf, wqkv[:, D:2 * D], BF16, epilogue="rope",
                               tables=tables)
            v_all = _nm_matmul(tok, layer, xall, n1, modf, wqkv[:, 2 * D:], BF16)
            g_x = _attention(B, S, C, heads, q_all, k_all, v_all, da_lambda[i_da],
                             da_subln_g[i_da].reshape(1, 2 * dh), lambda_init)
            g_c = None
            w_o = da_w_o[i_da].astype(BF16)
            i_da += 1
        if last:
            n_rows = B * S
            g_c = None
        else:
            n_rows = B * S + B * C
        x1, h2, scores_t = _proj_router(tok, layer, n_rows, g_x, g_c, w_o, xall, n2, modf, rw_hi, rw_lo)
        res = _moe(tok_dma, layer, scores_t, h2, x1, modf, router_bias,
                   moe_w_gate[layer], moe_w_up[layer], moe_w_down[layer],
                   final_g.reshape(1, D) if last else None)
        if last:
            out = res.reshape(B, S, D)
        else:
            xall = res
    return out
```

```python
import functools
import math

import jax
import jax.numpy as jnp
from jax import lax
from jax.experimental import pallas as pl
from jax.experimental.pallas import tpu as pltpu

F32 = jnp.float32
BF16 = jnp.bfloat16

EPS = 1e-6
GRID_W = 64
CONV_W = 4
CONV_LEFT = 2
RG_C = 8.0
ROPE_THETA = 10000.0
N_GROUPS = 4
N_MIXERS = 2

LANES = 128
SUBLANES = 8
VMEM_LIMIT = 56 * 1024 * 1024

ROW_TILE = 512
MOE_TILE = 256
DMA_TILE = 256
ROUTE_TILE = 1024
ATTN_TILE = 256
ATTN_KEY_GROUP = 2
SCAN_CHUNK = 256
COND_PAD = 8


def _cparams(n_axes):
    return pltpu.CompilerParams(dimension_semantics=("arbitrary",) * n_axes,
                                vmem_limit_bytes=VMEM_LIMIT)


def _adaln_kernel(cond_ref, w_ref, b_ref, o_ref):
    c = cond_ref[...]
    h = c * jax.nn.sigmoid(c)
    o_ref[0] = jnp.dot(h, w_ref[0], preferred_element_type=F32,
                       precision=lax.Precision.HIGHEST) + b_ref[0]


def _adaln(cond, w_mod, b_mod):
    depth, d, n6 = w_mod.shape
    tn = n6 // 6
    return pl.pallas_call(
        _adaln_kernel,
        grid=(depth, n6 // tn),
        in_specs=[pl.BlockSpec((COND_PAD, d), lambda l, j: (0, 0)),
                  pl.BlockSpec((1, d, tn), lambda l, j: (l, 0, j)),
                  pl.BlockSpec((1, 1, tn), lambda l, j: (l, 0, j))],
        out_specs=pl.BlockSpec((1, COND_PAD, tn), lambda l, j: (l, 0, j)),
        out_shape=jax.ShapeDtypeStruct((depth, COND_PAD, n6), F32),
        compiler_params=_cparams(2),
        name="adaln",
    )(cond, w_mod, b_mod.reshape(depth, 1, n6))


def _norm_mod(x, g, shift, scale):
    ms = jnp.mean(x * x, axis=-1, keepdims=True)
    h = (x * lax.rsqrt(ms + EPS)) * g
    return h * (1.0 + scale) + shift


class _Tok:
    def __init__(self, B, S, C, tile):
        assert S % tile == 0 and (B * C) % tile == 0
        self.B, self.S, self.C, self.tile = B, S, C, tile
        self.nx = B * S // tile
        self.nc = B * C // tile
        self.spt = S // tile

    def cond(self, i):
        return jnp.where(i < self.nx, i // self.spt, self.B)

    def mod_map(self, layer, chunk):
        return lambda i, *_: ((layer * COND_PAD + self.cond(i)) * 6 + chunk, 0, 0)


def _nm_matmul_kernel(x_ref, g_ref, sh_ref, sc_ref, w_ref, *rest, epilogue, out_scale):
    o_ref = rest[-1]
    h = _norm_mod(x_ref[...], g_ref[...], sh_ref[0], sc_ref[0])
    acc = jnp.dot(h.astype(BF16), w_ref[...], preferred_element_type=F32)
    if epilogue == "gelu":
        acc = jax.nn.gelu(acc, approximate=True)
        o_ref[...] = acc.astype(o_ref.dtype)
    elif epilogue == "rope":
        cos = rest[0][...]
        sin = rest[1][...]
        lane = lax.broadcasted_iota(jnp.int32, cos.shape, 1)
        first = (lane % 32) < 16
        for cb in range(acc.shape[1] // LANES):
            a = acc[:, cb * LANES:(cb + 1) * LANES]
            rot = jnp.where(first, pltpu.roll(a, LANES - 16, 1), pltpu.roll(a, 16, 1))
            r = (a * cos + rot * sin) * out_scale
            o_ref[:, cb * LANES:(cb + 1) * LANES] = r.astype(o_ref.dtype)
    else:
        o_ref[...] = acc.astype(o_ref.dtype)


def _nm_matmul(tok, layer, xall, g, modf, w, out_dtype, epilogue="none", tables=None, out_scale=1.0):
    ntok, d = xall.shape
    tm = tok.tile
    nw = w.shape[1]
    in_specs = [pl.BlockSpec((tm, d), lambda i: (i, 0)),
                pl.BlockSpec((1, d), lambda i: (0, 0)),
                pl.BlockSpec((1, 1, d), tok.mod_map(layer, 0)),
                pl.BlockSpec((1, 1, d), tok.mod_map(layer, 1)),
                pl.BlockSpec((d, nw), lambda i: (0, 0))]
    args = [xall, g, modf, modf, w]
    if epilogue == "rope":
        tmap = lambda i: (jnp.where(i < tok.nx, i % tok.spt, tok.spt), 0)
        in_specs += [pl.BlockSpec((tm, LANES), tmap), pl.BlockSpec((tm, LANES), tmap)]
        args += list(tables)
    return pl.pallas_call(
        functools.partial(_nm_matmul_kernel, epilogue=epilogue, out_scale=out_scale),
        grid=(ntok // tm,),
        in_specs=in_specs,
        out_specs=pl.BlockSpec((tm, nw), lambda i: (i, 0)),
        out_shape=jax.ShapeDtypeStruct((ntok, nw), out_dtype),
        compiler_params=_cparams(1),
        name="nm_matmul_" + epilogue,
    )(*args)


def _dot_nt(a, b):
    return lax.dot_general(a, b, (((1,), (1,)), ((), ())), preferred_element_type=F32)


def _nm_matmul_t_kernel(x_ref, g_ref, sh_ref, sc_ref, wt_ref, *rest, rope, out_scale):
    o_ref = rest[-1]
    h = _norm_mod(x_ref[...], g_ref[...], sh_ref[0], sc_ref[0])
    acc = _dot_nt(wt_ref[...], h.astype(BF16))
    n_c, nw, tc = o_ref.shape
    if rope:
        cos = rest[0][...]
        sin = rest[1][...]
        for g in range(nw // 32):
            a = acc[g * 32:g * 32 + 16]
            b = acc[g * 32 + 16:g * 32 + 32]
            c = cos[(g % 2) * 16:(g % 2) * 16 + 16]
            s = sin[(g % 2) * 16:(g % 2) * 16 + 16]
            ra = ((a * c - b * s) * out_scale).astype(o_ref.dtype)
            rb = ((b * c + a * s) * out_scale).astype(o_ref.dtype)
            for ci in range(n_c):
                o_ref[ci, g * 32:g * 32 + 16, :] = ra[:, ci * tc:(ci + 1) * tc]
                o_ref[ci, g * 32 + 16:g * 32 + 32, :] = rb[:, ci * tc:(ci + 1) * tc]
    else:
        for ci in range(n_c):
            o_ref[ci] = acc[:, ci * tc:(ci + 1) * tc].astype(o_ref.dtype)


def _nm_matmul_t(tok, layer, xall, g, modf, wt, tc, tables=None, out_scale=1.0):
    ntok, d = xall.shape
    tm = tok.tile
    nw = wt.shape[0]
    in_specs = [pl.BlockSpec((tm, d), lambda i: (i, 0)),
                pl.BlockSpec((1, d), lambda i: (0, 0)),
                pl.BlockSpec((1, 1, d), tok.mod_map(layer, 0)),
                pl.BlockSpec((1, 1, d), tok.mod_map(layer, 1)),
                pl.BlockSpec((nw, d), lambda i: (0, 0))]
    args = [xall, g, modf, modf, wt]
    if tables is not None:
        tmap = lambda i: (0, jnp.where(i < tok.nx, i % tok.spt, tok.spt))
        in_specs += [pl.BlockSpec((32, tm), tmap), pl.BlockSpec((32, tm), tmap)]
        args += list(tables)
    return pl.pallas_call(
        functools.partial(_nm_matmul_t_kernel, rope=tables is not None, out_scale=out_scale),
        grid=(ntok // tm,),
        in_specs=in_specs,
        out_specs=pl.BlockSpec((tm // tc, nw, tc), lambda i: (i, 0, 0)),
        out_shape=jax.ShapeDtypeStruct((ntok // tc, nw, tc), BF16),
        compiler_params=_cparams(1),
        name="nm_matmul_t_rope" if tables is not None else "nm_matmul_t",
    )(*args)


def _softplus(x):
    return jnp.maximum(x, 0.0) + jnp.log(1.0 + jnp.exp(-jnp.abs(x)))


def _rglru_kernel(ux_ref, uc_ref, yx_ref, yc_ref, cw_ref, cb_ref, wg_ref, gb_ref, lam_ref,
                  ox_ref, oc_ref, a_s, b_s, hf_s, hb_s, *, S, C, chunk):
    sp = _softplus(-lam_ref[...])
    cw = cw_ref[...]
    cb = cb_ref[...]
    wg = wg_ref[0]
    gb = gb_ref[0]

    def coeffs(u_ref, T):
        rc = min(chunk, T)

        def body(ci, _):
            c0 = pl.multiple_of(ci * rc, rc)
            cur = u_ref[pl.ds(c0, rc), :]
            pstart = pl.multiple_of(jnp.maximum(c0 - SUBLANES, 0), SUBLANES)
            nstart = pl.multiple_of(jnp.minimum(c0 + rc, T - SUBLANES), SUBLANES)
            prev = jnp.where(c0 > 0, u_ref[pl.ds(pstart, SUBLANES), :], 0.0)
            nxt = jnp.where(c0 + rc < T, u_ref[pl.ds(nstart, SUBLANES), :], 0.0)
            ext = jnp.concatenate([prev, cur, nxt], axis=0)
            n_ext = rc + 2 * SUBLANES
            xm2 = pltpu.roll(ext, 2, 0)[SUBLANES:SUBLANES + rc]
            xm1 = pltpu.roll(ext, 1, 0)[SUBLANES:SUBLANES + rc]
            xp1 = pltpu.roll(ext, n_ext - 1, 0)[SUBLANES:SUBLANES + rc]
            u = xm2 * cw[0:1] + xm1 * cw[1:2] + cur * cw[2:3] + xp1 * cw[3:4] + cb
            z = jnp.dot(u.astype(BF16), wg, preferred_element_type=F32) + gb
            for d in range(2):
                r = jax.nn.sigmoid(z[:, d * 2 * LANES:d * 2 * LANES + LANES])
                ig = jax.nn.sigmoid(z[:, d * 2 * LANES + LANES:(d + 1) * 2 * LANES])
                log_a = (-RG_C * r) * sp[d:d + 1]
                a = jnp.exp(log_a)
                bcoef = jnp.sqrt(1.0 - jnp.exp(2.0 * log_a)) * (ig * u)
                a_s[d, pl.ds(c0, rc), :] = a
                b_s[d, pl.ds(c0, rc), :] = bcoef
            return 0

        lax.fori_loop(0, T // rc, body, 0)

    row = lax.broadcasted_iota(jnp.int32, (SUBLANES, LANES), 0)

    def scans(T, cf, cbk):
        def body(k, carry):
            cf, cbk = carry
            rf = pl.multiple_of(k * SUBLANES, SUBLANES)
            rb = pl.multiple_of(T - SUBLANES - k * SUBLANES, SUBLANES)
            a = a_s[0, pl.ds(rf, SUBLANES), :]
            b = b_s[0, pl.ds(rf, SUBLANES), :]
            for s in (1, 2, 4):
                m = row >= s
                b = jnp.where(m, a * pltpu.roll(b, s, 0), 0.0) + b
                a = jnp.where(m, a * pltpu.roll(a, s, 0), a)
            h = b + a * cf
            hf_s[pl.ds(rf, SUBLANES), :] = h
            cf = h[SUBLANES - 1:SUBLANES, :]
            a = a_s[1, pl.ds(rb, SUBLANES), :]
            b = b_s[1, pl.ds(rb, SUBLANES), :]
            for s in (1, 2, 4):
                m = row < SUBLANES - s
                b = jnp.where(m, a * pltpu.roll(b, SUBLANES - s, 0), 0.0) + b
                a = jnp.where(m, a * pltpu.roll(a, SUBLANES - s, 0), a)
            h = b + a * cbk
            hb_s[pl.ds(rb, SUBLANES), :] = h
            cbk = h[0:1, :]
            return cf, cbk

        return lax.fori_loop(0, T // SUBLANES, body, (cf, cbk))

    zero = jnp.zeros((1, LANES), F32)
    coeffs(uc_ref, C)
    scans(C, zero, zero)
    oc_ref[...] = ((hf_s[0:C, :] + hb_s[0:C, :]) * yc_ref[...].astype(F32)).astype(oc_ref.dtype)
    h0f = hf_s[C - 1:C, :]
    h0b = hb_s[0:1, :]
    coeffs(ux_ref, S)
    scans(S, h0f, h0b)
    ox_ref[...] = ((hf_s[0:S, :] + hb_s[0:S, :]) * yx_ref[...].astype(F32)).astype(ox_ref.dtype)


def _rglru(B, S, C, u_all, y_all, conv_w, conv_b, wg, gb, lam):
    d_rnn = u_all.shape[1]
    nblk = d_rnn // LANES
    cbase = B * S // C
    tmax = max(S, C)
    return pl.pallas_call(
        functools.partial(_rglru_kernel, S=S, C=C, chunk=SCAN_CHUNK),
        grid=(B, nblk),
        in_specs=[pl.BlockSpec((S, LANES), lambda b, n: (b, n)),
                  pl.BlockSpec((C, LANES), lambda b, n: (cbase + b, n)),
                  pl.BlockSpec((S, LANES), lambda b, n: (b, n)),
                  pl.BlockSpec((C, LANES), lambda b, n: (cbase + b, n)),
                  pl.BlockSpec((CONV_W, LANES), lambda b, n: (0, n)),
                  pl.BlockSpec((1, LANES), lambda b, n: (0, n)),
                  pl.BlockSpec((1, LANES, 4 * LANES), lambda b, n: (n, 0, 0)),
                  pl.BlockSpec((1, 1, 4 * LANES), lambda b, n: (n, 0, 0)),
                  pl.BlockSpec((2, LANES), lambda b, n: (0, n))],
        out_specs=[pl.BlockSpec((S, LANES), lambda b, n: (b, n)),
                   pl.BlockSpec((C, LANES), lambda b, n: (b, n))],
        out_shape=[jax.ShapeDtypeStruct((B * S, d_rnn), BF16),
                   jax.ShapeDtypeStruct((B * C, d_rnn), BF16)],
        scratch_shapes=[pltpu.VMEM((2, tmax, LANES), F32), pltpu.VMEM((2, tmax, LANES), F32),
                        pltpu.VMEM((tmax, LANES), F32), pltpu.VMEM((tmax, LANES), F32)],
        compiler_params=_cparams(2),
        name="rglru",
    )(u_all, u_all, y_all, y_all, conv_w, conv_b, wg, gb, lam)


def _attn_kernel(q_ref, kx_ref, kc_ref, vx_ref, vc_ref, lam_ref, g_ref, o_ref, sa_ref, sb_ref, *, lambda_init):
    qt = q_ref[0]
    dh2, tq = qt.shape
    tk = kx_ref.shape[0] // vx_ref.shape[0]
    row = lax.broadcasted_iota(jnp.int32, qt.shape, 0)
    zero = jnp.zeros_like(qt)
    qz = jnp.concatenate([jnp.where(row < dh2 // 2, qt, zero), jnp.where(row >= dh2 // 2, qt, zero)], axis=1)

    def scores(kblk):
        return jnp.dot(kblk, qz, preferred_element_type=F32)

    def update(st, s, vt):
        m, l, acc = st
        mn = jnp.maximum(m, jnp.max(s, axis=0, keepdims=True))
        alpha = jnp.exp2(m - mn)
        p = jnp.exp2(s - mn)
        l = alpha * l + jnp.sum(p, axis=0, keepdims=True)
        acc = alpha * acc + jnp.dot(vt, p.astype(BF16), preferred_element_type=F32)
        return mn, l, acc

    st = (jnp.full((1, 2 * tq), -jnp.inf, F32), jnp.zeros((1, 2 * tq), F32), jnp.zeros((dh2, 2 * tq), F32))
    for j in range(vc_ref.shape[0]):
        st = update(st, scores(kc_ref[j * tk:(j + 1) * tk, :]), vc_ref[j])

    n_x = vx_ref.shape[0]
    group = ATTN_KEY_GROUP
    assert n_x % (2 * group) == 0
    gk = group * tk

    def xkeys(g):
        return kx_ref[pl.ds(pl.multiple_of(g * gk, gk), gk), :]

    def xvals(g):
        return jnp.concatenate([vx_ref[g * group + c] for c in range(group)], axis=1)

    sa_ref[...] = scores(xkeys(0))

    def pair(jj, st, last):
        g0 = 2 * jj
        sb_ref[...] = scores(xkeys(g0 + 1))
        st = update(st, sa_ref[...], xvals(g0))
        if not last:
            sa_ref[...] = scores(xkeys(g0 + 2))
        return update(st, sb_ref[...], xvals(g0 + 1))

    n_pairs = n_x // (2 * group)
    st = lax.fori_loop(0, n_pairs - 1, lambda jj, st: pair(jj, st, False), st)
    st = pair(n_pairs - 1, st, True)
    lv = lam_ref[...]
    lam = (jnp.exp(jnp.sum(lv[0:1] * lv[1:2], axis=-1, keepdims=True))
           - jnp.exp(jnp.sum(lv[2:3] * lv[3:4], axis=-1, keepdims=True)) + lambda_init)
    m, l, acc = st
    ot = acc[:, :tq] / l[:, :tq] - lam * (acc[:, tq:] / l[:, tq:])
    ms = jnp.mean(ot * ot, axis=0, keepdims=True)
    ot = (ot * lax.rsqrt(ms + EPS)) * g_ref[...] * (1.0 - lambda_init)
    o_ref[...] = ot.T.astype(o_ref.dtype)


def _attention(B, S, C, heads, qt_all, k_all, vt_all, lam, subln_g, lambda_init):
    d = k_all.shape[1]
    dh2 = d // heads
    tc = qt_all.shape[2]
    qpb = S // tc
    cbase = B * S // C
    return pl.pallas_call(
        functools.partial(_attn_kernel, lambda_init=lambda_init),
        grid=(B, heads, qpb),
        in_specs=[pl.BlockSpec((1, dh2, tc), lambda b, h, i: (b * qpb + i, h, 0)),
                  pl.BlockSpec((S, dh2), lambda b, h, i: (b, h)),
                  pl.BlockSpec((C, dh2), lambda b, h, i: (cbase + b, h)),
                  pl.BlockSpec((S // tc, dh2, tc), lambda b, h, i: (b, h, 0)),
                  pl.BlockSpec((C // tc, dh2, tc), lambda b, h, i: (cbase + b, h, 0)),
                  pl.BlockSpec(lam.shape, lambda b, h, i: (0, 0)),
                  pl.BlockSpec((dh2, 1), lambda b, h, i: (0, 0))],
        out_specs=pl.BlockSpec((tc, dh2), lambda b, h, i: (b * qpb + i, h)),
        out_shape=jax.ShapeDtypeStruct((B * S, d), BF16),
        scratch_shapes=[pltpu.VMEM((ATTN_KEY_GROUP * tc, 2 * tc), F32),
                        pltpu.VMEM((ATTN_KEY_GROUP * tc, 2 * tc), F32)],
        compiler_params=_cparams(3),
        name="diff_attn",
    )(qt_all, k_all, k_all, vt_all, vt_all, lam, subln_g)


def _proj_router_kernel(*refs, two_sources, nx):
    if two_sources:
        gx_ref, gc_ref = refs[:2]
        refs = refs[2:]
    else:
        gx_ref = refs[0]
        refs = refs[1:]
    (w_ref, x_ref, gate_ref, g2_ref, sh_ref, sc_ref, rwh_ref, rwl_ref, x1_ref, h2_ref, st_ref) = refs
    if two_sources:
        i = pl.program_id(0)
        gin = jnp.where(i < nx, gx_ref[...], gc_ref[...])
    else:
        gin = gx_ref[...]
    o = jnp.dot(gin, w_ref[...], preferred_element_type=F32)
    x1 = x_ref[...] + gate_ref[0] * o
    x1_ref[...] = x1
    h2 = _norm_mod(x1, g2_ref[...], sh_ref[0], sc_ref[0])
    h2_ref[...] = h2
    hi = h2.astype(BF16)
    lo = (h2 - hi.astype(F32)).astype(BF16)
    rwh = rwh_ref[...]
    logits = _dot_nt(rwh, hi) + _dot_nt(rwh, lo) + _dot_nt(rwl_ref[...], hi)
    st_ref[...] = jax.nn.sigmoid(logits)


def _proj_router(tok, layer, n_rows, g_x, g_c, w, xall, norm2_g, modf, rw_hi, rw_lo):
    d = xall.shape[1]
    tm = tok.tile
    e = rw_hi.shape[0]
    two = g_c is not None
    nxm1 = tok.nx - 1
    in_specs = []
    args = []
    if two:
        in_specs += [pl.BlockSpec((tm, d), lambda i: (jnp.minimum(i, nxm1), 0)),
                     pl.BlockSpec((tm, d), lambda i: (jnp.maximum(i - tok.nx, 0), 0))]
        args += [g_x, g_c]
    else:
        in_specs += [pl.BlockSpec((tm, d), lambda i: (i, 0))]
        args += [g_x]
    in_specs += [pl.BlockSpec((d, d), lambda i: (0, 0)),
                 pl.BlockSpec((tm, d), lambda i: (i, 0)),
                 pl.BlockSpec((1, 1, d), tok.mod_map(layer, 2)),
                 pl.BlockSpec((1, d), lambda i: (0, 0)),
                 pl.BlockSpec((1, 1, d), tok.mod_map(layer, 3)),
                 pl.BlockSpec((1, 1, d), tok.mod_map(layer, 4)),
                 pl.BlockSpec((e, d), lambda i: (0, 0)),
                 pl.BlockSpec((e, d), lambda i: (0, 0))]
    args += [w, xall, modf, norm2_g, modf, modf, rw_hi, rw_lo]
    return pl.pallas_call(
        functools.partial(_proj_router_kernel, two_sources=two, nx=tok.nx),
        grid=(n_rows // tm,),
        in_specs=in_specs,
        out_specs=[pl.BlockSpec((tm, d), lambda i: (i, 0)),
                   pl.BlockSpec((tm, d), lambda i: (i, 0)),
                   pl.BlockSpec((e, tm), lambda i: (0, i))],
        out_shape=[jax.ShapeDtypeStruct((n_rows, d), F32),
                   jax.ShapeDtypeStruct((n_rows, d), F32),
                   jax.ShapeDtypeStruct((e, n_rows), F32)],
        compiler_params=_cparams(1),
        name="proj_router",
    )(*args)


def _route_kernel(s_ref, bias_ref, u_ref, ints_ref, wts_ref, cnt_ref, carry_ref, *, n_groups):
    i = pl.program_id(0)

    @pl.when(i == 0)
    def _():
        carry_ref[...] = jnp.zeros_like(carry_ref)

    s = s_ref[...]
    n_e, tr = s.shape
    epg = n_e // n_groups
    sel = s + bias_ref[...]
    ridx = lax.broadcasted_iota(jnp.int32, (epg, tr), 0).astype(F32)
    big = float(epg)
    best = gi = bi1 = bi2 = None
    for g in range(n_groups):
        sg = sel[g * epg:(g + 1) * epg, :]
        m1 = jnp.max(sg, axis=0, keepdims=True)
        i1 = jnp.min(jnp.where(sg == m1, ridx, big), axis=0, keepdims=True)
        sg2 = jnp.where(ridx == i1, -jnp.inf, sg)
        m2 = jnp.max(sg2, axis=0, keepdims=True)
        i2 = jnp.min(jnp.where(sg2 == m2, ridx, big), axis=0, keepdims=True)
        score = m1 + m2
        if g == 0:
            best, gi, bi1, bi2 = score, jnp.zeros_like(i1), i1, i2
        else:
            better = score > best
            best = jnp.where(better, score, best)
            gi = jnp.where(better, float(g), gi)
            bi1 = jnp.where(better, i1, bi1)
            bi2 = jnp.where(better, i2, bi2)
    e0 = gi * epg + bi1
    e1 = gi * epg + bi2
    eidx = lax.broadcasted_iota(jnp.int32, (n_e, tr), 0).astype(F32)
    oh0 = eidx == e0
    oh1 = eidx == e1
    w0 = jnp.sum(jnp.where(oh0, s, 0.0), axis=0, keepdims=True)
    w1 = jnp.sum(jnp.where(oh1, s, 0.0), axis=0, keepdims=True)
    den = w0 + w1
    wts_ref[0:1, :] = w0 / den
    wts_ref[1:2, :] = w1 / den
    f0 = jnp.where(oh0, 1.0, 0.0)
    f1 = jnp.where(oh1, 1.0, 0.0)
    u = u_ref[...]
    p0 = jnp.dot(f0.astype(BF16), u, preferred_element_type=F32)
    p1 = jnp.dot(f1.astype(BF16), u, preferred_element_type=F32)
    carry = carry_ref[...]
    c0 = jnp.sum(f0, axis=1, keepdims=True)
    c1 = jnp.sum(f1, axis=1, keepdims=True)
    rank0 = jnp.sum(f0 * (carry + p0), axis=0, keepdims=True)
    rank1 = jnp.sum(f1 * (carry + c0 + p1), axis=0, keepdims=True)
    ints_ref[0:1, :] = e0.astype(jnp.int32)
    ints_ref[1:2, :] = e1.astype(jnp.int32)
    ints_ref[2:3, :] = rank0.astype(jnp.int32)
    ints_ref[3:4, :] = rank1.astype(jnp.int32)
    carry = carry + c0 + c1
    carry_ref[...] = carry
    cnt_ref[...] = jnp.broadcast_to(carry, cnt_ref.shape)


def _route(scores_t, bias):
    n_e, nt = scores_t.shape
    tr = max(t for t in range(LANES, min(ROUTE_TILE, nt) + 1, LANES) if nt % t == 0)
    tri = jnp.triu(jnp.ones((tr, tr), BF16), k=1)
    return pl.pallas_call(
        functools.partial(_route_kernel, n_groups=N_GROUPS),
        grid=(nt // tr,),
        in_specs=[pl.BlockSpec((n_e, tr), lambda i: (0, i)),
                  pl.BlockSpec((n_e, 1), lambda i: (0, 0)),
                  pl.BlockSpec((tr, tr), lambda i: (0, 0))],
        out_specs=[pl.BlockSpec((4, tr), lambda i: (0, i)),
                   pl.BlockSpec((2, tr), lambda i: (0, i)),
                   pl.BlockSpec((n_e, LANES), lambda i: (0, 0))],
        out_shape=[jax.ShapeDtypeStruct((4, nt), jnp.int32),
                   jax.ShapeDtypeStruct((2, nt), F32),
                   jax.ShapeDtypeStruct((n_e, LANES), F32)],
        scratch_shapes=[pltpu.VMEM((n_e, 1), F32)],
        compiler_params=_cparams(1),
        name="route",
    )(scores_t, bias.reshape(n_e, 1), tri)


def _dispatch_kernel(pstart_ref, ints_ref, h_ref, xs_in_ref, xs_ref, sem, *, td):
    del xs_in_ref

    def start(t, _):
        for k in range(2):
            dst = pstart_ref[ints_ref[k, t]] + ints_ref[2 + k, t]
            pltpu.make_async_copy(h_ref.at[pl.ds(t, 1), :], xs_ref.at[pl.ds(dst, 1), :], sem).start()
        return 0

    lax.fori_loop(0, td, start, 0)

    def wait(t, _):
        for k in range(2):
            pltpu.make_async_copy(h_ref.at[pl.ds(0, 1), :], xs_ref.at[pl.ds(0, 1), :], sem).wait()
        return 0

    lax.fori_loop(0, td, wait, 0)


def _dispatch(pstart, ints, h2, n_slots):
    nt, d = h2.shape
    td = min(DMA_TILE, nt)
    xs0 = jnp.zeros((n_slots, d), h2.dtype)
    gs = pltpu.PrefetchScalarGridSpec(
        num_scalar_prefetch=1,
        grid=(nt // td,),
        in_specs=[pl.BlockSpec((4, td), lambda i, p: (0, i), memory_space=pltpu.SMEM),
                  pl.BlockSpec((td, d), lambda i, p: (i, 0)),
                  pl.BlockSpec(memory_space=pl.ANY)],
        out_specs=pl.BlockSpec(memory_space=pl.ANY),
        scratch_shapes=[pltpu.SemaphoreType.DMA(())],
    )
    return pl.pallas_call(
        functools.partial(_dispatch_kernel, td=td),
        grid_spec=gs,
        out_shape=jax.ShapeDtypeStruct((n_slots, d), h2.dtype),
        input_output_aliases={3: 0},
        compiler_params=_cparams(1),
        name="moe_dispatch",
    )(pstart, ints, h2, xs0)


def _ffn_kernel(be_ref, bi_ref, nu_ref, xs_ref, wg_ref, wu_ref, wd_ref, y_ref, wgb, wub, wdb):
    j = pl.program_id(0)

    @pl.when(j < nu_ref[0])
    def _():
        prev = be_ref[jnp.maximum(j - 1, 0)]
        changed = jnp.logical_or(j == 0, be_ref[j] != prev)

        @pl.when(changed)
        def _():
            wgb[...] = wg_ref[0, 0].astype(BF16)
            wub[...] = wu_ref[0, 0].astype(BF16)
            wdb[...] = wd_ref[0, 0].astype(BF16)

        x = xs_ref[...].astype(BF16)
        g = jnp.dot(x, wgb[...], preferred_element_type=F32)
        u = jnp.dot(x, wub[...], preferred_element_type=F32)
        h = (g * jax.nn.sigmoid(g)) * u
        y_ref[...] = jnp.dot(h.astype(BF16), wdb[...], preferred_element_type=F32)

    @pl.when(j >= nu_ref[0])
    def _():
        y_ref[...] = jnp.zeros_like(y_ref)


def _ffn(layer, blk_e, blk_in, n_used, xs, w_gate, w_up, w_down):
    n_slots, d = xs.shape
    de = w_gate.shape[-1]
    tb = MOE_TILE
    nb = n_slots // tb
    gs = pltpu.PrefetchScalarGridSpec(
        num_scalar_prefetch=3,
        grid=(nb,),
        in_specs=[pl.BlockSpec((tb, d), lambda j, be, bi, nu: (bi[j], 0)),
                  pl.BlockSpec((1, 1, d, de), lambda j, be, bi, nu: (layer, be[j], 0, 0)),
                  pl.BlockSpec((1, 1, d, de), lambda j, be, bi, nu: (layer, be[j], 0, 0)),
                  pl.BlockSpec((1, 1, de, d), lambda j, be, bi, nu: (layer, be[j], 0, 0))],
        out_specs=pl.BlockSpec((tb, d), lambda j, be, bi, nu: (j, 0)),
        scratch_shapes=[pltpu.VMEM((d, de), BF16), pltpu.VMEM((d, de), BF16), pltpu.VMEM((de, d), BF16)],
    )
    return pl.pallas_call(
        _ffn_kernel,
        grid_spec=gs,
        out_shape=jax.ShapeDtypeStruct((n_slots, d), F32),
        compiler_params=_cparams(1),
        name="moe_ffn",
    )(blk_e, blk_in, n_used, xs, w_gate, w_up, w_down)


def _combine_kernel(pstart_ref, ints_ref, wts_ref, x1_ref, gate_ref, *rest, td, final):
    if final:
        fg_ref, yb_ref, o_ref, ybuf, sem = rest
    else:
        yb_ref, o_ref, ybuf, sem = rest

    def start(t, _):
        for k in range(2):
            src = pstart_ref[ints_ref[k, t]] + ints_ref[2 + k, t]
            pltpu.make_async_copy(yb_ref.at[pl.ds(src, 1), :], ybuf.at[k, pl.ds(t, 1), :], sem).start()
        return 0

    lax.fori_loop(0, td, start, 0)

    def wait(t, _):
        for k in range(2):
            pltpu.make_async_copy(yb_ref.at[pl.ds(0, 1), :], ybuf.at[k, pl.ds(0, 1), :], sem).wait()
        return 0

    lax.fori_loop(0, td, wait, 0)
    w = wts_ref[...]
    y = ybuf[0] * w[:, 0:1] + ybuf[1] * w[:, 1:2]
    xn = x1_ref[...] + gate_ref[0] * y
    if final:
        ms = jnp.mean(xn * xn, axis=-1, keepdims=True)
        xn = (xn * lax.rsqrt(ms + EPS)) * fg_ref[...]
    o_ref[...] = xn


def _combine(tok, layer, pstart, ints, wts_rows, x1, modf, yb, final_g):
    nt, d = x1.shape
    td = tok.tile
    final = final_g is not None
    in_specs = [pl.BlockSpec((4, td), lambda i, p: (0, i), memory_space=pltpu.SMEM),
                pl.BlockSpec((td, 2), lambda i, p: (i, 0)),
                pl.BlockSpec((td, d), lambda i, p: (i, 0)),
                pl.BlockSpec((1, 1, d), tok.mod_map(layer, 5))]
    args = [ints, wts_rows, x1, modf]
    if final:
        in_specs.append(pl.BlockSpec((1, d), lambda i, p: (0, 0)))
        args.append(final_g)
    in_specs.append(pl.BlockSpec(memory_space=pl.ANY))
    args.append(yb)
    gs = pltpu.PrefetchScalarGridSpec(
        num_scalar_prefetch=1,
        grid=(nt // td,),
        in_specs=in_specs,
        out_specs=pl.BlockSpec((td, d), lambda i, p: (i, 0)),
        scratch_shapes=[pltpu.VMEM((2, td, d), F32), pltpu.SemaphoreType.DMA(())],
    )
    return pl.pallas_call(
        functools.partial(_combine_kernel, td=td, final=final),
        grid_spec=gs,
        out_shape=jax.ShapeDtypeStruct((nt, d), F32),
        compiler_params=_cparams(1),
        name="moe_combine",
    )(pstart, *args)


def _moe(tok_dma, layer, scores_t, h2, x1, modf, router_bias, w_gate, w_up, w_down, final_g):
    nt = h2.shape[0]
    n_e = w_gate.shape[1]
    tb = MOE_TILE
    ints, wts, cnt = _route(scores_t, router_bias.astype(F32))
    counts = cnt[:, 0].astype(jnp.int32)
    padded = ((counts + tb - 1) // tb) * tb
    p_end = jnp.cumsum(padded)
    p_start = (p_end - padded).astype(jnp.int32)
    nb = (2 * nt) // tb + n_e
    n_used = (p_end[-1] // tb).astype(jnp.int32)
    jb = jnp.arange(nb, dtype=jnp.int32)
    be = jnp.sum((p_end[None, :] <= (jb * tb)[:, None]).astype(jnp.int32), axis=1)
    be = jnp.minimum(be, n_e - 1)
    last = jnp.maximum(n_used - 1, 0)
    blk_e = jnp.where(jb < n_used, be, be[last])
    blk_in = jnp.minimum(jb, last)
    xs = _dispatch(p_start, ints, h2, nb * tb)
    yb = _ffn(layer, blk_e, blk_in, n_used.reshape(1), xs, w_gate, w_up, w_down)
    return _combine(tok_dma, layer, p_start, ints, wts.T, x1, modf, yb, final_g)


def _rope_tables(S, tile):
    n_rows = S // GRID_W
    row = jnp.repeat(jnp.arange(n_rows), GRID_W).astype(F32)
    col = jnp.tile(jnp.arange(GRID_W), n_rows).astype(F32)
    n_freq = 16
    inv = 1.0 / (ROPE_THETA ** (jnp.arange(n_freq, dtype=F32) / n_freq))
    ang = jnp.stack([row, col], axis=-1)[:, :, None] * inv
    ang = jnp.broadcast_to(ang[:, :, None, :], (S, 2, 2, n_freq)).reshape(S, 4 * n_freq)
    cos, sin = jnp.cos(ang), jnp.sin(ang)
    pick = jnp.array(list(range(n_freq)) + list(range(2 * n_freq, 3 * n_freq)))
    cos_t = jnp.concatenate([cos[:, pick].T, jnp.ones((2 * n_freq, tile), F32)], axis=1)
    sin_t = jnp.concatenate([sin[:, pick].T, jnp.zeros((2 * n_freq, tile), F32)], axis=1)
    sign = jnp.where((jnp.arange(4 * n_freq) % 32) < 16, -1.0, 1.0).astype(F32)
    sin = sin * sign
    cos = jnp.concatenate([jnp.tile(cos, (1, 2)), jnp.ones((tile, LANES), F32)], axis=0)
    sin = jnp.concatenate([jnp.tile(sin, (1, 2)), jnp.zeros((tile, LANES), F32)], axis=0)
    return (cos, sin), (cos_t, sin_t)


def kernel(x, c, ctx, c_ctx, w_mod, b_mod, norm1_g, norm2_g, rg_w_in, rg_conv_w, rg_conv_b,
           rg_gate_a_w, rg_gate_a_b, rg_gate_x_w, rg_gate_x_b, rg_lambda, rg_w_out,
           da_w_qkv, da_lambda, da_subln_g, da_w_o, router_w, router_bias,
           moe_w_gate, moe_w_up, moe_w_down, final_g):
    B, S, D = x.shape
    C = ctx.shape[1]
    depth = w_mod.shape[0]
    assert B + 1 <= COND_PAD and D % LANES == 0
    assert all(l % N_MIXERS == 0 or l == depth - 1 for l in range(depth))
    tok = _Tok(B, S, C, min(ROW_TILE, S, B * C))
    tok_dma = _Tok(B, S, C, min(DMA_TILE, S, B * C))

    cond = jnp.zeros((COND_PAD, D), F32).at[:B].set(c).at[B].set(c_ctx)
    mods = _adaln(cond, w_mod, b_mod)
    modf = mods.reshape(depth * COND_PAD * 6, 1, D)
    xall = jnp.concatenate([x.reshape(B * S, D), ctx.reshape(B * C, D)], axis=0)
    rw_t = router_w.T.astype(F32)
    rw_hi = rw_t.astype(BF16)
    rw_lo = (rw_t - rw_hi.astype(F32)).astype(BF16)

    i_rg = 0
    i_da = 0
    out = None
    for layer in range(depth):
        last = layer == depth - 1
        n1 = norm1_g[layer].reshape(1, D)
        n2 = norm2_g[layer].reshape(1, D)
        if layer % N_MIXERS == 0:
            d_rnn = rg_w_in.shape[2] // 2
            w_in = rg_w_in[i_rg].astype(BF16)
            y_all = _nm_matmul(tok, layer, xall, n1, modf, w_in[:, :d_rnn], BF16, epilogue="gelu")
            u_all = _nm_matmul(tok, layer, xall, n1, modf, w_in[:, d_rnn:], F32)
            nblk = rg_gate_a_w.shape[2]
            wg = jnp.concatenate([rg_gate_a_w[i_rg, 0], rg_gate_x_w[i_rg, 0],
                                  rg_gate_a_w[i_rg, 1], rg_gate_x_w[i_rg, 1]], axis=-1).astype(BF16)
            bsplit = lambda v: v.reshape(nblk, 1, LANES)
            gb = jnp.concatenate([bsplit(rg_gate_a_b[i_rg, 0]), bsplit(rg_gate_x_b[i_rg, 0]),
                                  bsplit(rg_gate_a_b[i_rg, 1]), bsplit(rg_gate_x_b[i_rg, 1])], axis=-1)
            g_x, g_c = _rglru(B, S, C, u_all, y_all, rg_conv_w[i_rg], rg_conv_b[i_rg].reshape(1, d_rnn),
                              wg, gb, rg_lambda[i_rg])
            w_o = rg_w_out[i_rg].astype(BF16)
            i_rg += 1
        else:
            dh = da_lambda.shape[2]
            heads = D // (2 * dh)
            lambda_init = 0.8 - 0.6 * math.exp(-0.3 * layer)
            wqkv = da_w_qkv[i_da].astype(BF16)
            tables, tables_t = _rope_tables(S, tok.tile)
            tc = min(ATTN_TILE, S, C)
            assert S % tc == 0 and C % tc == 0 and tok.tile % tc == 0
            qt_all = _nm_matmul_t(tok, layer, xall, n1, modf, wqkv[:, :D].T, tc, tables=tables_t,
                                  out_scale=math.log2(math.e) * dh ** -0.5)
            k_all = _nm_matmul(tok, layer, xall, n1, modf, wqkv[:, D:2 * D], BF16, epilogue="rope",
                               tables=tables)
            vt_all = _nm_matmul_t(tok, layer, xall, n1, modf, wqkv[:, 2 * D:].T, tc)
            g_x = _attention(B, S, C, heads, qt_all, k_all, vt_all, da_lambda[i_da],
                             da_subln_g[i_da].reshape(2 * dh, 1), lambda_init)
            g_c = None
            w_o = da_w_o[i_da].astype(BF16)
            i_da += 1
        if last:
            n_rows = B * S
            g_c = None
        else:
            n_rows = B * S + B * C
        x1, h2, scores_t = _proj_router(tok, layer, n_rows, g_x, g_c, w_o, xall, n2, modf, rw_hi, rw_lo)
        res = _moe(tok_dma, layer, scores_t, h2, x1, modf, router_bias,
                   moe_w_gate, moe_w_up, moe_w_down,
                   final_g.reshape(1, D) if last else None)
        if last:
            out = res.reshape(B, S, D)
        else:
            xall = res
    return out
```

```python
import functools
import math

import jax
import jax.numpy as jnp
from jax import lax
from jax.experimental import pallas as pl
from jax.experimental.pallas import tpu as pltpu

F32 = jnp.float32
BF16 = jnp.bfloat16

EPS = 1e-6
GRID_W = 64
CONV_W = 4
CONV_LEFT = 2
RG_C = 8.0
ROPE_THETA = 10000.0
N_GROUPS = 4
N_MIXERS = 2

LANES = 128
SUBLANES = 8
VMEM_LIMIT = 56 * 1024 * 1024

ROW_TILE = 512
MOE_TILE = 256
DMA_TILE = 512
DMA_UNROLL = 8
WAIT_UNROLL = 32
ROUTE_TILE = 1024
ATTN_Q_TILE = 256
ATTN_KEY_TILE = 256
ATTN_KEY_GROUP = 2
ATTN_SHIFT_SLACK = 64.0
SCAN_CHUNK = 256
COND_PAD = 8


def _cparams(n_axes):
    return pltpu.CompilerParams(dimension_semantics=("arbitrary",) * n_axes,
                                vmem_limit_bytes=VMEM_LIMIT)


def _adaln_kernel(cond_ref, w_ref, b_ref, o_ref):
    c = cond_ref[...]
    h = c * jax.nn.sigmoid(c)
    o_ref[0] = jnp.dot(h, w_ref[0], preferred_element_type=F32,
                       precision=lax.Precision.HIGHEST) + b_ref[0]


def _adaln(cond, w_mod, b_mod):
    depth, d, n6 = w_mod.shape
    tn = n6 // 6
    return pl.pallas_call(
        _adaln_kernel,
        grid=(depth, n6 // tn),
        in_specs=[pl.BlockSpec((COND_PAD, d), lambda l, j: (0, 0)),
                  pl.BlockSpec((1, d, tn), lambda l, j: (l, 0, j)),
                  pl.BlockSpec((1, 1, tn), lambda l, j: (l, 0, j))],
        out_specs=pl.BlockSpec((1, COND_PAD, tn), lambda l, j: (l, 0, j)),
        out_shape=jax.ShapeDtypeStruct((depth, COND_PAD, n6), F32),
        compiler_params=_cparams(2),
        name="adaln",
    )(cond, w_mod, b_mod.reshape(depth, 1, n6))


def _norm_mod(x, g, shift, scale):
    ms = jnp.mean(x * x, axis=-1, keepdims=True)
    h = (x * lax.rsqrt(ms + EPS)) * g
    return h * (1.0 + scale) + shift


class _Tok:
    def __init__(self, B, S, C, tile):
        assert S % tile == 0 and (B * C) % tile == 0
        self.B, self.S, self.C, self.tile = B, S, C, tile
        self.nx = B * S // tile
        self.nc = B * C // tile
        self.spt = S // tile

    def cond(self, i):
        return jnp.where(i < self.nx, i // self.spt, self.B)

    def mod_map(self, layer, chunk):
        return lambda i, *_: ((layer * COND_PAD + self.cond(i)) * 6 + chunk, 0, 0)


def _nm_matmul_kernel(x_ref, g_ref, sh_ref, sc_ref, w_ref, *rest, epilogue, out_scale):
    o_ref = rest[-1]
    h = _norm_mod(x_ref[...], g_ref[...], sh_ref[0], sc_ref[0])
    acc = jnp.dot(h.astype(BF16), w_ref[...], preferred_element_type=F32)
    if epilogue == "gelu":
        acc = jax.nn.gelu(acc, approximate=True)
        o_ref[...] = acc.astype(o_ref.dtype)
    elif epilogue == "rope":
        cos = rest[0][...]
        sin = rest[1][...]
        lane = lax.broadcasted_iota(jnp.int32, cos.shape, 1)
        first = (lane % 32) < 16
        for cb in range(acc.shape[1] // LANES):
            a = acc[:, cb * LANES:(cb + 1) * LANES]
            rot = jnp.where(first, pltpu.roll(a, LANES - 16, 1), pltpu.roll(a, 16, 1))
            r = (a * cos + rot * sin) * out_scale
            o_ref[:, cb * LANES:(cb + 1) * LANES] = r.astype(o_ref.dtype)
    else:
        o_ref[...] = acc.astype(o_ref.dtype)


def _nm_matmul(tok, layer, xall, g, modf, w, out_dtype, epilogue="none", tables=None, out_scale=1.0):
    ntok, d = xall.shape
    tm = tok.tile
    nw = w.shape[1]
    in_specs = [pl.BlockSpec((tm, d), lambda i: (i, 0)),
                pl.BlockSpec((1, d), lambda i: (0, 0)),
                pl.BlockSpec((1, 1, d), tok.mod_map(layer, 0)),
                pl.BlockSpec((1, 1, d), tok.mod_map(layer, 1)),
                pl.BlockSpec((d, nw), lambda i: (0, 0))]
    args = [xall, g, modf, modf, w]
    if epilogue == "rope":
        tmap = lambda i: (jnp.where(i < tok.nx, i % tok.spt, tok.spt), 0)
        in_specs += [pl.BlockSpec((tm, LANES), tmap), pl.BlockSpec((tm, LANES), tmap)]
        args += list(tables)
    return pl.pallas_call(
        functools.partial(_nm_matmul_kernel, epilogue=epilogue, out_scale=out_scale),
        grid=(ntok // tm,),
        in_specs=in_specs,
        out_specs=pl.BlockSpec((tm, nw), lambda i: (i, 0)),
        out_shape=jax.ShapeDtypeStruct((ntok, nw), out_dtype),
        compiler_params=_cparams(1),
        name="nm_matmul_" + epilogue,
    )(*args)


def _dot_nt(a, b):
    return lax.dot_general(a, b, (((1,), (1,)), ((), ())), preferred_element_type=F32)


def _nm_matmul_t_kernel(x_ref, g_ref, sh_ref, sc_ref, wt_ref, *rest, rope, out_scale):
    o_ref = rest[-1]
    h = _norm_mod(x_ref[...], g_ref[...], sh_ref[0], sc_ref[0])
    acc = _dot_nt(wt_ref[...], h.astype(BF16))
    n_c, nw, tc = o_ref.shape
    if rope:
        cos = rest[0][...]
        sin = rest[1][...]
        for g in range(nw // 32):
            a = acc[g * 32:g * 32 + 16]
            b = acc[g * 32 + 16:g * 32 + 32]
            c = cos[(g % 2) * 16:(g % 2) * 16 + 16]
            s = sin[(g % 2) * 16:(g % 2) * 16 + 16]
            ra = ((a * c - b * s) * out_scale).astype(o_ref.dtype)
            rb = ((b * c + a * s) * out_scale).astype(o_ref.dtype)
            for ci in range(n_c):
                o_ref[ci, g * 32:g * 32 + 16, :] = ra[:, ci * tc:(ci + 1) * tc]
                o_ref[ci, g * 32 + 16:g * 32 + 32, :] = rb[:, ci * tc:(ci + 1) * tc]
    else:
        for ci in range(n_c):
            o_ref[ci] = acc[:, ci * tc:(ci + 1) * tc].astype(o_ref.dtype)


def _nm_matmul_t(tok, layer, xall, g, modf, wt, tc, tables=None, out_scale=1.0):
    ntok, d = xall.shape
    tm = tok.tile
    nw = wt.shape[0]
    in_specs = [pl.BlockSpec((tm, d), lambda i: (i, 0)),
                pl.BlockSpec((1, d), lambda i: (0, 0)),
                pl.BlockSpec((1, 1, d), tok.mod_map(layer, 0)),
                pl.BlockSpec((1, 1, d), tok.mod_map(layer, 1)),
                pl.BlockSpec((nw, d), lambda i: (0, 0))]
    args = [xall, g, modf, modf, wt]
    if tables is not None:
        tmap = lambda i: (0, jnp.where(i < tok.nx, i % tok.spt, tok.spt))
        in_specs += [pl.BlockSpec((32, tm), tmap), pl.BlockSpec((32, tm), tmap)]
        args += list(tables)
    return pl.pallas_call(
        functools.partial(_nm_matmul_t_kernel, rope=tables is not None, out_scale=out_scale),
        grid=(ntok // tm,),
        in_specs=in_specs,
        out_specs=pl.BlockSpec((tm // tc, nw, tc), lambda i: (i, 0, 0)),
        out_shape=jax.ShapeDtypeStruct((ntok // tc, nw, tc), BF16),
        compiler_params=_cparams(1),
        name="nm_matmul_t_rope" if tables is not None else "nm_matmul_t",
    )(*args)


def _softplus(x):
    return jnp.maximum(x, 0.0) + jnp.log(1.0 + jnp.exp(-jnp.abs(x)))


def _rglru_kernel(ux_ref, uc_ref, yx_ref, yc_ref, cw_ref, cb_ref, wg_ref, gb_ref, lam_ref,
                  ox_ref, oc_ref, a_s, b_s, hf_s, hb_s, *, S, C, chunk):
    sp = _softplus(-lam_ref[...])
    cw = cw_ref[...]
    cb = cb_ref[...]
    wg = wg_ref[0]
    gb = gb_ref[0]

    def coeffs(u_ref, T):
        rc = min(chunk, T)

        def body(ci, _):
            c0 = pl.multiple_of(ci * rc, rc)
            cur = u_ref[pl.ds(c0, rc), :]
            pstart = pl.multiple_of(jnp.maximum(c0 - SUBLANES, 0), SUBLANES)
            nstart = pl.multiple_of(jnp.minimum(c0 + rc, T - SUBLANES), SUBLANES)
            prev = jnp.where(c0 > 0, u_ref[pl.ds(pstart, SUBLANES), :], 0.0)
            nxt = jnp.where(c0 + rc < T, u_ref[pl.ds(nstart, SUBLANES), :], 0.0)
            ext = jnp.concatenate([prev, cur, nxt], axis=0)
            n_ext = rc + 2 * SUBLANES
            xm2 = pltpu.roll(ext, 2, 0)[SUBLANES:SUBLANES + rc]
            xm1 = pltpu.roll(ext, 1, 0)[SUBLANES:SUBLANES + rc]
            xp1 = pltpu.roll(ext, n_ext - 1, 0)[SUBLANES:SUBLANES + rc]
            u = xm2 * cw[0:1] + xm1 * cw[1:2] + cur * cw[2:3] + xp1 * cw[3:4] + cb
            z = jnp.dot(u.astype(BF16), wg, preferred_element_type=F32) + gb
            for d in range(2):
                r = jax.nn.sigmoid(z[:, d * 2 * LANES:d * 2 * LANES + LANES])
                ig = jax.nn.sigmoid(z[:, d * 2 * LANES + LANES:(d + 1) * 2 * LANES])
                log_a = (-RG_C * r) * sp[d:d + 1]
                a = jnp.exp(log_a)
                bcoef = jnp.sqrt(1.0 - jnp.exp(2.0 * log_a)) * (ig * u)
                a_s[d, pl.ds(c0, rc), :] = a
                b_s[d, pl.ds(c0, rc), :] = bcoef
            return 0

        lax.fori_loop(0, T // rc, body, 0)

    row = lax.broadcasted_iota(jnp.int32, (SUBLANES, LANES), 0)

    def scans(T, cf, cbk):
        def body(k, carry):
            cf, cbk = carry
            rf = pl.multiple_of(k * SUBLANES, SUBLANES)
            rb = pl.multiple_of(T - SUBLANES - k * SUBLANES, SUBLANES)
            a = a_s[0, pl.ds(rf, SUBLANES), :]
            b = b_s[0, pl.ds(rf, SUBLANES), :]
            for s in (1, 2, 4):
                m = row >= s
                b = jnp.where(m, a * pltpu.roll(b, s, 0), 0.0) + b
                a = jnp.where(m, a * pltpu.roll(a, s, 0), a)
            h = b + a * cf
            hf_s[pl.ds(rf, SUBLANES), :] = h
            cf = h[SUBLANES - 1:SUBLANES, :]
            a = a_s[1, pl.ds(rb, SUBLANES), :]
            b = b_s[1, pl.ds(rb, SUBLANES), :]
            for s in (1, 2, 4):
                m = row < SUBLANES - s
                b = jnp.where(m, a * pltpu.roll(b, SUBLANES - s, 0), 0.0) + b
                a = jnp.where(m, a * pltpu.roll(a, SUBLANES - s, 0), a)
            h = b + a * cbk
            hb_s[pl.ds(rb, SUBLANES), :] = h
            cbk = h[0:1, :]
            return cf, cbk

        return lax.fori_loop(0, T // SUBLANES, body, (cf, cbk))

    zero = jnp.zeros((1, LANES), F32)
    coeffs(uc_ref, C)
    scans(C, zero, zero)
    oc_ref[...] = ((hf_s[0:C, :] + hb_s[0:C, :]) * yc_ref[...].astype(F32)).astype(oc_ref.dtype)
    h0f = hf_s[C - 1:C, :]
    h0b = hb_s[0:1, :]
    coeffs(ux_ref, S)
    scans(S, h0f, h0b)
    ox_ref[...] = ((hf_s[0:S, :] + hb_s[0:S, :]) * yx_ref[...].astype(F32)).astype(ox_ref.dtype)


def _rglru(B, S, C, u_all, y_all, conv_w, conv_b, wg, gb, lam):
    d_rnn = u_all.shape[1]
    nblk = d_rnn // LANES
    cbase = B * S // C
    tmax = max(S, C)
    return pl.pallas_call(
        functools.partial(_rglru_kernel, S=S, C=C, chunk=SCAN_CHUNK),
        grid=(B, nblk),
        in_specs=[pl.BlockSpec((S, LANES), lambda b, n: (b, n)),
                  pl.BlockSpec((C, LANES), lambda b, n: (cbase + b, n)),
                  pl.BlockSpec((S, LANES), lambda b, n: (b, n)),
                  pl.BlockSpec((C, LANES), lambda b, n: (cbase + b, n)),
                  pl.BlockSpec((CONV_W, LANES), lambda b, n: (0, n)),
                  pl.BlockSpec((1, LANES), lambda b, n: (0, n)),
                  pl.BlockSpec((1, LANES, 4 * LANES), lambda b, n: (n, 0, 0)),
                  pl.BlockSpec((1, 1, 4 * LANES), lambda b, n: (n, 0, 0)),
                  pl.BlockSpec((2, LANES), lambda b, n: (0, n))],
        out_specs=[pl.BlockSpec((S, LANES), lambda b, n: (b, n)),
                   pl.BlockSpec((C, LANES), lambda b, n: (b, n))],
        out_shape=[jax.ShapeDtypeStruct((B * S, d_rnn), BF16),
                   jax.ShapeDtypeStruct((B * C, d_rnn), BF16)],
        scratch_shapes=[pltpu.VMEM((2, tmax, LANES), F32), pltpu.VMEM((2, tmax, LANES), F32),
                        pltpu.VMEM((tmax, LANES), F32), pltpu.VMEM((tmax, LANES), F32)],
        compiler_params=_cparams(2),
        name="rglru",
    )(u_all, u_all, y_all, y_all, conv_w, conv_b, wg, gb, lam)


def _attn_kernel(q_ref, kx_ref, kc_ref, vx_ref, vc_ref, lam_ref, g_ref, o_ref, sa_ref, sb_ref, *, lambda_init):
    qt = q_ref[0]
    dh2, tq = qt.shape
    tk = kx_ref.shape[0] // vx_ref.shape[0]
    row = lax.broadcasted_iota(jnp.int32, qt.shape, 0)
    zero = jnp.zeros_like(qt)
    qz = jnp.concatenate([jnp.where(row < dh2 // 2, qt, zero), jnp.where(row >= dh2 // 2, qt, zero)], axis=1)

    def scores(kblk):
        return jnp.dot(kblk, qz, preferred_element_type=F32)

    def update(st, s, vt):
        m, l, acc = st
        mn = jnp.maximum(m, jnp.max(s, axis=0, keepdims=True))
        alpha = jnp.exp2(m - mn)
        p = jnp.exp2(s - mn)
        l = alpha * l + jnp.sum(p, axis=0, keepdims=True)
        acc = alpha * acc + jnp.dot(vt, p.astype(BF16), preferred_element_type=F32)
        return mn, l, acc

    n_x = vx_ref.shape[0]
    group = ATTN_KEY_GROUP
    assert n_x % group == 0 and 0 < vc_ref.shape[0] <= group
    n_g = n_x // group
    gk = group * tk
    n_c = vc_ref.shape[0] * tk

    def xkeys(g):
        return kx_ref[g * gk:(g + 1) * gk, :]

    def xvals(g):
        return jnp.concatenate([vx_ref[g * group + c] for c in range(group)], axis=1)

    vct = jnp.concatenate([vc_ref[c] for c in range(vc_ref.shape[0])], axis=1)
    lv = lam_ref[...]
    lam = (jnp.exp(jnp.sum(lv[0:1] * lv[1:2], axis=-1, keepdims=True))
           - jnp.exp(jnp.sum(lv[2:3] * lv[3:4], axis=-1, keepdims=True)) + lambda_init)

    def finish(l, acc):
        ot = acc[:, :tq] / l[:, :tq] - lam * (acc[:, tq:] / l[:, tq:])
        ms = jnp.mean(ot * ot, axis=0, keepdims=True)
        ot = (ot * lax.rsqrt(ms + EPS)) * g_ref[...] * (1.0 - lambda_init)
        o_ref[...] = ot.T.astype(o_ref.dtype)

    bufs = (sa_ref, sb_ref)
    s = scores(kc_ref[...])
    bufs[0][...] = scores(xkeys(0))
    m = jnp.max(s, axis=0, keepdims=True)
    p = jnp.exp2(s - m)
    l = jnp.sum(p, axis=0, keepdims=True)
    acc = jnp.dot(vct, p.astype(BF16), preferred_element_type=F32)
    top = m
    for g in range(n_g):
        if g + 1 < n_g:
            bufs[(g + 1) % 2][...] = scores(xkeys(g + 1))
        s = bufs[g % 2][...]
        top = jnp.maximum(top, jnp.max(s, axis=0, keepdims=True))
        p = jnp.exp2(s - m)
        l = l + jnp.sum(p, axis=0, keepdims=True)
        acc = acc + jnp.dot(xvals(g), p.astype(BF16), preferred_element_type=F32)
    finish(l, acc)
    excess = jnp.max(top - m)

    @pl.when(jnp.logical_not(excess <= ATTN_SHIFT_SLACK))
    def _():
        st = (jnp.full((1, 2 * tq), -jnp.inf, F32), jnp.zeros((1, 2 * tq), F32), jnp.zeros((dh2, 2 * tq), F32))
        bufs = (sa_ref, sb_ref)
        bufs[0][...] = scores(xkeys(0))
        for g in range(n_g):
            nxt = bufs[(g + 1) % 2]
            if g + 1 < n_g:
                nxt[...] = scores(xkeys(g + 1))
            else:
                nxt[0:n_c, :] = scores(kc_ref[...])
            st = update(st, bufs[g % 2][...], xvals(g))
        st = update(st, bufs[n_g % 2][0:n_c, :], vct)
        finish(st[1], st[2])


def _attention(B, S, C, heads, qt_all, k_all, vt_all, lam, subln_g, lambda_init):
    d = k_all.shape[1]
    dh2 = d // heads
    tq = qt_all.shape[2]
    tc = vt_all.shape[2]
    qpb = S // tq
    cbase = B * S // C
    return pl.pallas_call(
        functools.partial(_attn_kernel, lambda_init=lambda_init),
        grid=(B, heads, qpb),
        in_specs=[pl.BlockSpec((1, dh2, tq), lambda b, h, i: (b * qpb + i, h, 0)),
                  pl.BlockSpec((S, dh2), lambda b, h, i: (b, h)),
                  pl.BlockSpec((C, dh2), lambda b, h, i: (cbase + b, h)),
                  pl.BlockSpec((S // tc, dh2, tc), lambda b, h, i: (b, h, 0)),
                  pl.BlockSpec((C // tc, dh2, tc), lambda b, h, i: (cbase + b, h, 0)),
                  pl.BlockSpec(lam.shape, lambda b, h, i: (0, 0)),
                  pl.BlockSpec((dh2, 1), lambda b, h, i: (0, 0))],
        out_specs=pl.BlockSpec((tq, dh2), lambda b, h, i: (b * qpb + i, h)),
        out_shape=jax.ShapeDtypeStruct((B * S, d), BF16),
        scratch_shapes=[pltpu.VMEM((ATTN_KEY_GROUP * tc, 2 * tq), F32),
                        pltpu.VMEM((ATTN_KEY_GROUP * tc, 2 * tq), F32)],
        compiler_params=_cparams(3),
        name="diff_attn",
    )(qt_all, k_all, k_all, vt_all, vt_all, lam, subln_g)


def _proj_router_kernel(*refs, two_sources, nx):
    if two_sources:
        gx_ref, gc_ref = refs[:2]
        refs = refs[2:]
    else:
        gx_ref = refs[0]
        refs = refs[1:]
    (w_ref, x_ref, gate_ref, g2_ref, sh_ref, sc_ref, rwh_ref, rwl_ref, x1_ref, h2_ref, st_ref) = refs
    if two_sources:
        i = pl.program_id(0)
        gin = jnp.where(i < nx, gx_ref[...], gc_ref[...])
    else:
        gin = gx_ref[...]
    o = jnp.dot(gin, w_ref[...], preferred_element_type=F32)
    x1 = x_ref[...] + gate_ref[0] * o
    x1_ref[...] = x1
    h2 = _norm_mod(x1, g2_ref[...], sh_ref[0], sc_ref[0])
    h2_ref[...] = h2
    hi = h2.astype(BF16)
    lo = (h2 - hi.astype(F32)).astype(BF16)
    rwh = rwh_ref[...]
    logits = _dot_nt(rwh, hi) + _dot_nt(rwh, lo) + _dot_nt(rwl_ref[...], hi)
    st_ref[...] = jax.nn.sigmoid(logits)


def _proj_router(tok, layer, n_rows, g_x, g_c, w, xall, norm2_g, modf, rw_hi, rw_lo):
    d = xall.shape[1]
    tm = tok.tile
    e = rw_hi.shape[0]
    two = g_c is not None
    nxm1 = tok.nx - 1
    in_specs = []
    args = []
    if two:
        in_specs += [pl.BlockSpec((tm, d), lambda i: (jnp.minimum(i, nxm1), 0)),
                     pl.BlockSpec((tm, d), lambda i: (jnp.maximum(i - tok.nx, 0), 0))]
        args += [g_x, g_c]
    else:
        in_specs += [pl.BlockSpec((tm, d), lambda i: (i, 0))]
        args += [g_x]
    in_specs += [pl.BlockSpec((d, d), lambda i: (0, 0)),
                 pl.BlockSpec((tm, d), lambda i: (i, 0)),
                 pl.BlockSpec((1, 1, d), tok.mod_map(layer, 2)),
                 pl.BlockSpec((1, d), lambda i: (0, 0)),
                 pl.BlockSpec((1, 1, d), tok.mod_map(layer, 3)),
                 pl.BlockSpec((1, 1, d), tok.mod_map(layer, 4)),
                 pl.BlockSpec((e, d), lambda i: (0, 0)),
                 pl.BlockSpec((e, d), lambda i: (0, 0))]
    args += [w, xall, modf, norm2_g, modf, modf, rw_hi, rw_lo]
    return pl.pallas_call(
        functools.partial(_proj_router_kernel, two_sources=two, nx=tok.nx),
        grid=(n_rows // tm,),
        in_specs=in_specs,
        out_specs=[pl.BlockSpec((tm, d), lambda i: (i, 0)),
                   pl.BlockSpec((tm, d), lambda i: (i, 0)),
                   pl.BlockSpec((e, tm), lambda i: (0, i))],
        out_shape=[jax.ShapeDtypeStruct((n_rows, d), F32),
                   jax.ShapeDtypeStruct((n_rows, d), F32),
                   jax.ShapeDtypeStruct((e, n_rows), F32)],
        compiler_params=_cparams(1),
        name="proj_router",
    )(*args)


def _route_kernel(s_ref, bias_ref, u_ref, ints_ref, wts_ref, cnt_ref, carry_ref, *, n_groups):
    i = pl.program_id(0)

    @pl.when(i == 0)
    def _():
        carry_ref[...] = jnp.zeros_like(carry_ref)

    s = s_ref[...]
    n_e, tr = s.shape
    epg = n_e // n_groups
    sel = s + bias_ref[...]
    ridx = lax.broadcasted_iota(jnp.int32, (epg, tr), 0).astype(F32)
    big = float(epg)
    best = gi = bi1 = bi2 = None
    for g in range(n_groups):
        sg = sel[g * epg:(g + 1) * epg, :]
        m1 = jnp.max(sg, axis=0, keepdims=True)
        i1 = jnp.min(jnp.where(sg == m1, ridx, big), axis=0, keepdims=True)
        sg2 = jnp.where(ridx == i1, -jnp.inf, sg)
        m2 = jnp.max(sg2, axis=0, keepdims=True)
        i2 = jnp.min(jnp.where(sg2 == m2, ridx, big), axis=0, keepdims=True)
        score = m1 + m2
        if g == 0:
            best, gi, bi1, bi2 = score, jnp.zeros_like(i1), i1, i2
        else:
            better = score > best
            best = jnp.where(better, score, best)
            gi = jnp.where(better, float(g), gi)
            bi1 = jnp.where(better, i1, bi1)
            bi2 = jnp.where(better, i2, bi2)
    e0 = gi * epg + bi1
    e1 = gi * epg + bi2
    eidx = lax.broadcasted_iota(jnp.int32, (n_e, tr), 0).astype(F32)
    oh0 = eidx == e0
    oh1 = eidx == e1
    w0 = jnp.sum(jnp.where(oh0, s, 0.0), axis=0, keepdims=True)
    w1 = jnp.sum(jnp.where(oh1, s, 0.0), axis=0, keepdims=True)
    den = w0 + w1
    wts_ref[0:1, :] = w0 / den
    wts_ref[1:2, :] = w1 / den
    f0 = jnp.where(oh0, 1.0, 0.0)
    f1 = jnp.where(oh1, 1.0, 0.0)
    u = u_ref[...]
    p0 = jnp.dot(f0.astype(BF16), u, preferred_element_type=F32)
    p1 = jnp.dot(f1.astype(BF16), u, preferred_element_type=F32)
    carry = carry_ref[...]
    c0 = jnp.sum(f0, axis=1, keepdims=True)
    c1 = jnp.sum(f1, axis=1, keepdims=True)
    rank0 = jnp.sum(f0 * (carry + p0), axis=0, keepdims=True)
    rank1 = jnp.sum(f1 * (carry + c0 + p1), axis=0, keepdims=True)
    ints_ref[0:1, :] = e0.astype(jnp.int32)
    ints_ref[1:2, :] = e1.astype(jnp.int32)
    ints_ref[2:3, :] = rank0.astype(jnp.int32)
    ints_ref[3:4, :] = rank1.astype(jnp.int32)
    carry = carry + c0 + c1
    carry_ref[...] = carry
    cnt_ref[...] = jnp.broadcast_to(carry, cnt_ref.shape)


def _route(scores_t, bias):
    n_e, nt = scores_t.shape
    tr = max(t for t in range(LANES, min(ROUTE_TILE, nt) + 1, LANES) if nt % t == 0)
    tri = jnp.triu(jnp.ones((tr, tr), BF16), k=1)
    return pl.pallas_call(
        functools.partial(_route_kernel, n_groups=N_GROUPS),
        grid=(nt // tr,),
        in_specs=[pl.BlockSpec((n_e, tr), lambda i: (0, i)),
                  pl.BlockSpec((n_e, 1), lambda i: (0, 0)),
                  pl.BlockSpec((tr, tr), lambda i: (0, 0))],
        out_specs=[pl.BlockSpec((4, tr), lambda i: (0, i)),
                   pl.BlockSpec((2, tr), lambda i: (0, i)),
                   pl.BlockSpec((n_e, LANES), lambda i: (0, 0))],
        out_shape=[jax.ShapeDtypeStruct((4, nt), jnp.int32),
                   jax.ShapeDtypeStruct((2, nt), F32),
                   jax.ShapeDtypeStruct((n_e, LANES), F32)],
        scratch_shapes=[pltpu.VMEM((n_e, 1), F32)],
        compiler_params=_cparams(1),
        name="route",
    )(scores_t, bias.reshape(n_e, 1), tri)


def _row_dmas(n_rows, make_copy, sem_wait_copy):
    def start(tb, _):
        for u in range(DMA_UNROLL):
            t = tb * DMA_UNROLL + u
            for k in range(2):
                make_copy(t, k).start(priority=(2 * u + k) % 2)
        return 0

    lax.fori_loop(0, n_rows // DMA_UNROLL, start, 0)

    def wait(tb, _):
        for _u in range(2 * WAIT_UNROLL):
            sem_wait_copy().wait()
        return 0

    lax.fori_loop(0, n_rows // WAIT_UNROLL, wait, 0)


def _dispatch_kernel(fill_ref, dest_ref, h_ref, xs_ref, zeros_ref, sem, fill_sem, *, td, n_e, tb):
    _row_dmas(td,
              lambda t, k: pltpu.make_async_copy(h_ref.at[pl.ds(t, 1), :],
                                                 xs_ref.at[pl.ds(dest_ref[k, t], 1), :], sem),
              lambda: pltpu.make_async_copy(h_ref.at[pl.ds(0, 1), :], xs_ref.at[pl.ds(0, 1), :], sem))

    @pl.when(pl.program_id(0) == 0)
    def _():
        zeros_ref[...] = jnp.zeros_like(zeros_ref)
        bits = [1 << b for b in range(tb.bit_length() - 2, 2, -1)]

        def run(phase, pred, n_rows, off):
            cp = pltpu.make_async_copy(zeros_ref.at[pl.ds(0, n_rows), :], xs_ref.at[pl.ds(off, n_rows), :],
                                       fill_sem)

            @pl.when(pred)
            def _():
                if phase == "start":
                    cp.start()
                else:
                    cp.wait()

        half = tb // 2
        n_half = xs_ref.shape[0] // half

        def tail(phase):
            def body(j, _):
                cp = pltpu.make_async_copy(zeros_ref, xs_ref.at[pl.ds(pl.multiple_of(j * half, half), half), :],
                                           fill_sem)
                if phase == "start":
                    cp.start()
                else:
                    cp.wait()
                return 0

            lax.fori_loop(fill_ref[0, n_e], n_half, body, 0)

        for phase in ("start", "wait"):
            for e in range(n_e):
                first, n_single, off, n = (fill_ref[r, e] for r in range(4))
                for r in range(SUBLANES - 1):
                    run(phase, r < n_single, 1, first + r)
                for bit in bits:
                    run(phase, (n & bit) != 0, bit, pl.multiple_of(off, SUBLANES))
                    off = off + (n & bit)
            tail(phase)


def _dispatch(fill, dest, h2, n_slots, n_e):
    nt, d = h2.shape
    td = min(DMA_TILE, nt)
    tb = MOE_TILE
    gs = pltpu.PrefetchScalarGridSpec(
        num_scalar_prefetch=1,
        grid=(nt // td,),
        in_specs=[pl.BlockSpec((2, td), lambda i, f: (0, i), memory_space=pltpu.SMEM),
                  pl.BlockSpec((td, d), lambda i, f: (i, 0))],
        out_specs=pl.BlockSpec(memory_space=pl.ANY),
        scratch_shapes=[pltpu.VMEM((tb // 2, d), h2.dtype), pltpu.SemaphoreType.DMA(()),
                        pltpu.SemaphoreType.DMA(())],
    )
    return pl.pallas_call(
        functools.partial(_dispatch_kernel, td=td, n_e=n_e, tb=tb),
        grid_spec=gs,
        out_shape=jax.ShapeDtypeStruct((n_slots, d), h2.dtype),
        compiler_params=_cparams(1),
        name="moe_dispatch",
    )(fill, dest, h2)


def _ffn_kernel(be_ref, bi_ref, nu_ref, xs_ref, wg_ref, wu_ref, wd_ref, y_ref, wgb, wub, wdb):
    j = pl.program_id(0)

    @pl.when(j < nu_ref[0])
    def _():
        prev = be_ref[jnp.maximum(j - 1, 0)]
        changed = jnp.logical_or(j == 0, be_ref[j] != prev)

        @pl.when(changed)
        def _():
            wgb[...] = wg_ref[0, 0].astype(BF16)
            wub[...] = wu_ref[0, 0].astype(BF16)
            wdb[...] = wd_ref[0, 0].astype(BF16)

        x = xs_ref[...].astype(BF16)
        g = jnp.dot(x, wgb[...], preferred_element_type=F32)
        u = jnp.dot(x, wub[...], preferred_element_type=F32)
        h = (g * jax.nn.sigmoid(g)) * u
        y_ref[...] = jnp.dot(h.astype(BF16), wdb[...], preferred_element_type=F32)

    @pl.when(j >= nu_ref[0])
    def _():
        y_ref[...] = jnp.zeros_like(y_ref)


def _ffn(layer, blk_e, blk_in, n_used, xs, w_gate, w_up, w_down):
    n_slots, d = xs.shape
    de = w_gate.shape[-1]
    tb = MOE_TILE
    nb = n_slots // tb
    gs = pltpu.PrefetchScalarGridSpec(
        num_scalar_prefetch=3,
        grid=(nb,),
        in_specs=[pl.BlockSpec((tb, d), lambda j, be, bi, nu: (bi[j], 0)),
                  pl.BlockSpec((1, 1, d, de), lambda j, be, bi, nu: (layer, be[j], 0, 0)),
                  pl.BlockSpec((1, 1, d, de), lambda j, be, bi, nu: (layer, be[j], 0, 0)),
                  pl.BlockSpec((1, 1, de, d), lambda j, be, bi, nu: (layer, be[j], 0, 0))],
        out_specs=pl.BlockSpec((tb, d), lambda j, be, bi, nu: (j, 0)),
        scratch_shapes=[pltpu.VMEM((d, de), BF16), pltpu.VMEM((d, de), BF16), pltpu.VMEM((de, d), BF16)],
    )
    return pl.pallas_call(
        _ffn_kernel,
        grid_spec=gs,
        out_shape=jax.ShapeDtypeStruct((n_slots, d), F32),
        compiler_params=_cparams(1),
        name="moe_ffn",
    )(blk_e, blk_in, n_used, xs, w_gate, w_up, w_down)


def _combine_kernel(dest_ref, wts_ref, x1_ref, gate_ref, *rest, td, final):
    if final:
        fg_ref, yb_ref, o_ref, ybuf, sem = rest
    else:
        yb_ref, o_ref, ybuf, sem = rest

    _row_dmas(td,
              lambda t, k: pltpu.make_async_copy(yb_ref.at[pl.ds(dest_ref[k, t], 1), :],
                                                 ybuf.at[k, pl.ds(t, 1), :], sem),
              lambda: pltpu.make_async_copy(yb_ref.at[pl.ds(0, 1), :], ybuf.at[0, pl.ds(0, 1), :], sem))
    w = wts_ref[...]
    y = ybuf[0] * w[:, 0:1] + ybuf[1] * w[:, 1:2]
    xn = x1_ref[...] + gate_ref[0] * y
    if final:
        ms = jnp.mean(xn * xn, axis=-1, keepdims=True)
        xn = (xn * lax.rsqrt(ms + EPS)) * fg_ref[...]
    o_ref[...] = xn


def _combine(tok, layer, dest, wts_rows, x1, modf, yb, final_g):
    nt, d = x1.shape
    td = tok.tile
    final = final_g is not None
    in_specs = [pl.BlockSpec((2, td), lambda i: (0, i), memory_space=pltpu.SMEM),
                pl.BlockSpec((td, 2), lambda i: (i, 0)),
                pl.BlockSpec((td, d), lambda i: (i, 0)),
                pl.BlockSpec((1, 1, d), tok.mod_map(layer, 5))]
    args = [dest, wts_rows, x1, modf]
    if final:
        in_specs.append(pl.BlockSpec((1, d), lambda i: (0, 0)))
        args.append(final_g)
    in_specs.append(pl.BlockSpec(memory_space=pl.ANY))
    args.append(yb)
    return pl.pallas_call(
        functools.partial(_combine_kernel, td=td, final=final),
        grid=(nt // td,),
        in_specs=in_specs,
        out_specs=pl.BlockSpec((td, d), lambda i: (i, 0)),
        out_shape=jax.ShapeDtypeStruct((nt, d), F32),
        scratch_shapes=[pltpu.VMEM((2, td, d), F32), pltpu.SemaphoreType.DMA(())],
        compiler_params=_cparams(1),
        name="moe_combine",
    )(*args)


def _moe(tok_dma, layer, scores_t, h2, x1, modf, router_bias, w_gate, w_up, w_down, final_g):
    nt = h2.shape[0]
    n_e = w_gate.shape[1]
    tb = MOE_TILE
    ints, wts, cnt = _route(scores_t, router_bias.astype(F32))
    counts = cnt[:, 0].astype(jnp.int32)
    padded = ((counts + tb - 1) // tb) * tb
    p_end = jnp.cumsum(padded)
    p_start = (p_end - padded).astype(jnp.int32)
    nb = (2 * nt) // tb + n_e
    n_used = (p_end[-1] // tb).astype(jnp.int32)
    jb = jnp.arange(nb, dtype=jnp.int32)
    be = jnp.sum((p_end[None, :] <= (jb * tb)[:, None]).astype(jnp.int32), axis=1)
    be = jnp.minimum(be, n_e - 1)
    last = jnp.maximum(n_used - 1, 0)
    blk_e = jnp.where(jb < n_used, be, be[last])
    blk_in = jnp.minimum(jb, last)
    dest = p_start[ints[0:2]] + ints[2:4]
    pad0 = p_start + counts
    pad8 = jnp.minimum(((pad0 + SUBLANES - 1) // SUBLANES) * SUBLANES, p_end)
    fill = jnp.stack([pad0, pad8 - pad0, pad8, p_end - pad8]).astype(jnp.int32)
    fill = jnp.concatenate([fill, jnp.full((4, 1), 2 * n_used, jnp.int32)], axis=1)
    xs = _dispatch(fill, dest, h2, nb * tb, n_e)
    yb = _ffn(layer, blk_e, blk_in, n_used.reshape(1), xs, w_gate, w_up, w_down)
    return _combine(tok_dma, layer, dest, wts.T, x1, modf, yb, final_g)


def _rope_tables(S, tile):
    n_rows = S // GRID_W
    row = jnp.repeat(jnp.arange(n_rows), GRID_W).astype(F32)
    col = jnp.tile(jnp.arange(GRID_W), n_rows).astype(F32)
    n_freq = 16
    inv = 1.0 / (ROPE_THETA ** (jnp.arange(n_freq, dtype=F32) / n_freq))
    ang = jnp.stack([row, col], axis=-1)[:, :, None] * inv
    ang = jnp.broadcast_to(ang[:, :, None, :], (S, 2, 2, n_freq)).reshape(S, 4 * n_freq)
    cos, sin = jnp.cos(ang), jnp.sin(ang)
    pick = jnp.array(list(range(n_freq)) + list(range(2 * n_freq, 3 * n_freq)))
    cos_t = jnp.concatenate([cos[:, pick].T, jnp.ones((2 * n_freq, tile), F32)], axis=1)
    sin_t = jnp.concatenate([sin[:, pick].T, jnp.zeros((2 * n_freq, tile), F32)], axis=1)
    sign = jnp.where((jnp.arange(4 * n_freq) % 32) < 16, -1.0, 1.0).astype(F32)
    sin = sin * sign
    cos = jnp.concatenate([jnp.tile(cos, (1, 2)), jnp.ones((tile, LANES), F32)], axis=0)
    sin = jnp.concatenate([jnp.tile(sin, (1, 2)), jnp.zeros((tile, LANES), F32)], axis=0)
    return (cos, sin), (cos_t, sin_t)


def kernel(x, c, ctx, c_ctx, w_mod, b_mod, norm1_g, norm2_g, rg_w_in, rg_conv_w, rg_conv_b,
           rg_gate_a_w, rg_gate_a_b, rg_gate_x_w, rg_gate_x_b, rg_lambda, rg_w_out,
           da_w_qkv, da_lambda, da_subln_g, da_w_o, router_w, router_bias,
           moe_w_gate, moe_w_up, moe_w_down, final_g):
    B, S, D = x.shape
    C = ctx.shape[1]
    depth = w_mod.shape[0]
    assert B + 1 <= COND_PAD and D % LANES == 0
    assert all(l % N_MIXERS == 0 or l == depth - 1 for l in range(depth))
    tok = _Tok(B, S, C, min(ROW_TILE, S, B * C))
    tok_dma = _Tok(B, S, C, min(DMA_TILE, S, B * C))

    cond = jnp.zeros((COND_PAD, D), F32).at[:B].set(c).at[B].set(c_ctx)
    mods = _adaln(cond, w_mod, b_mod)
    modf = mods.reshape(depth * COND_PAD * 6, 1, D)
    xall = jnp.concatenate([x.reshape(B * S, D), ctx.reshape(B * C, D)], axis=0)
    rw_t = router_w.T.astype(F32)
    rw_hi = rw_t.astype(BF16)
    rw_lo = (rw_t - rw_hi.astype(F32)).astype(BF16)

    i_rg = 0
    i_da = 0
    out = None
    for layer in range(depth):
        last = layer == depth - 1
        n1 = norm1_g[layer].reshape(1, D)
        n2 = norm2_g[layer].reshape(1, D)
        if layer % N_MIXERS == 0:
            d_rnn = rg_w_in.shape[2] // 2
            w_in = rg_w_in[i_rg].astype(BF16)
            y_all = _nm_matmul(tok, layer, xall, n1, modf, w_in[:, :d_rnn], BF16, epilogue="gelu")
            u_all = _nm_matmul(tok, layer, xall, n1, modf, w_in[:, d_rnn:], F32)
            nblk = rg_gate_a_w.shape[2]
            wg = jnp.concatenate([rg_gate_a_w[i_rg, 0], rg_gate_x_w[i_rg, 0],
                                  rg_gate_a_w[i_rg, 1], rg_gate_x_w[i_rg, 1]], axis=-1).astype(BF16)
            bsplit = lambda v: v.reshape(nblk, 1, LANES)
            gb = jnp.concatenate([bsplit(rg_gate_a_b[i_rg, 0]), bsplit(rg_gate_x_b[i_rg, 0]),
                                  bsplit(rg_gate_a_b[i_rg, 1]), bsplit(rg_gate_x_b[i_rg, 1])], axis=-1)
            g_x, g_c = _rglru(B, S, C, u_all, y_all, rg_conv_w[i_rg], rg_conv_b[i_rg].reshape(1, d_rnn),
                              wg, gb, rg_lambda[i_rg])
            w_o = rg_w_out[i_rg].astype(BF16)
            i_rg += 1
        else:
            dh = da_lambda.shape[2]
            heads = D // (2 * dh)
            lambda_init = 0.8 - 0.6 * math.exp(-0.3 * layer)
            wqkv = da_w_qkv[i_da].astype(BF16)
            tables, tables_t = _rope_tables(S, tok.tile)
            tc = min(ATTN_KEY_TILE, S, C)
            tq = min(ATTN_Q_TILE, S, tok.tile)
            assert S % tc == 0 and C % tc == 0 and tok.tile % tc == 0 and S % tq == 0 and tok.tile % tq == 0
            qt_all = _nm_matmul_t(tok, layer, xall, n1, modf, wqkv[:, :D].T, tq, tables=tables_t,
                                  out_scale=math.log2(math.e) * dh ** -0.5)
            k_all = _nm_matmul(tok, layer, xall, n1, modf, wqkv[:, D:2 * D], BF16, epilogue="rope",
                               tables=tables)
            vt_all = _nm_matmul_t(tok, layer, xall, n1, modf, wqkv[:, 2 * D:].T, tc)
            g_x = _attention(B, S, C, heads, qt_all, k_all, vt_all, da_lambda[i_da],
                             da_subln_g[i_da].reshape(2 * dh, 1), lambda_init)
            g_c = None
            w_o = da_w_o[i_da].astype(BF16)
            i_da += 1
        if last:
            n_rows = B * S
            g_c = None
        else:
            n_rows = B * S + B * C
        x1, h2, scores_t = _proj_router(tok, layer, n_rows, g_x, g_c, w_o, xall, n2, modf, rw_hi, rw_lo)
        res = _moe(tok_dma, layer, scores_t, h2, x1, modf, router_bias,
                   moe_w_gate, moe_w_up, moe_w_down,
                   final_g.reshape(1, D) if last else None)
        if last:
            out = res.reshape(B, S, D)
        else:
            xall = res
    return out
```

```python
import functools
import math

import jax
import jax.numpy as jnp
from jax import lax
from jax.experimental import pallas as pl
from jax.experimental.pallas import tpu as pltpu

F32 = jnp.float32
BF16 = jnp.bfloat16

EPS = 1e-6
TINY = 1e-30
GRID_W = 64
CONV_W = 4
CONV_LEFT = 2
RG_C = 8.0
ROPE_THETA = 10000.0
N_GROUPS = 4
N_MIXERS = 2

LANES = 128
SUBLANES = 8
VMEM_LIMIT = 56 * 1024 * 1024

ROW_TILE = 512
MOE_TILE = 256
DMA_TILE = 512
DMA_UNROLL = 8
WAIT_UNROLL = 32
ROUTE_TILE = 1024
ATTN_Q_TILE = 256
ATTN_KEY_TILE = 256
ATTN_KEY_GROUP = 2
ATTN_HEADS_PER_STEP = 2
ATTN_SHIFT_SLACK = 64.0
SCAN_CHUNK = 256
SCAN_UNROLL = 8
COND_PAD = 8


def _cparams(n_axes):
    return pltpu.CompilerParams(dimension_semantics=("arbitrary",) * n_axes,
                                vmem_limit_bytes=VMEM_LIMIT)


def _adaln_kernel(cond_ref, w_ref, b_ref, o_ref):
    c = cond_ref[...]
    h = c * jax.nn.sigmoid(c)
    o_ref[0] = jnp.dot(h, w_ref[0], preferred_element_type=F32,
                       precision=lax.Precision.HIGHEST) + b_ref[0]


def _adaln(cond, w_mod, b_mod):
    depth, d, n6 = w_mod.shape
    tn = n6 // 6
    return pl.pallas_call(
        _adaln_kernel,
        grid=(depth, n6 // tn),
        in_specs=[pl.BlockSpec((COND_PAD, d), lambda l, j: (0, 0)),
                  pl.BlockSpec((1, d, tn), lambda l, j: (l, 0, j)),
                  pl.BlockSpec((1, 1, tn), lambda l, j: (l, 0, j))],
        out_specs=pl.BlockSpec((1, COND_PAD, tn), lambda l, j: (l, 0, j)),
        out_shape=jax.ShapeDtypeStruct((depth, COND_PAD, n6), F32),
        compiler_params=_cparams(2),
        name="adaln",
    )(cond, w_mod, b_mod.reshape(depth, 1, n6))


def _norm_mod(x, g, shift, scale):
    ms = jnp.mean(x * x, axis=-1, keepdims=True)
    h = (x * lax.rsqrt(ms + EPS)) * g
    return h * (1.0 + scale) + shift


class _Tok:
    def __init__(self, B, S, C, tile):
        assert S % tile == 0 and (B * C) % tile == 0
        self.B, self.S, self.C, self.tile = B, S, C, tile
        self.nx = B * S // tile
        self.nc = B * C // tile
        self.spt = S // tile

    def cond(self, i):
        return jnp.where(i < self.nx, i // self.spt, self.B)

    def mod_map(self, layer, chunk):
        return lambda i, *_: ((layer * COND_PAD + self.cond(i)) * 6 + chunk, 0, 0)


def _nm_matmul_kernel(x_ref, g_ref, sh_ref, sc_ref, w_ref, *rest, epilogue, out_scale):
    o_ref = rest[-1]
    h = _norm_mod(x_ref[...], g_ref[...], sh_ref[0], sc_ref[0])
    acc = jnp.dot(h.astype(BF16), w_ref[...], preferred_element_type=F32)
    if epilogue == "gelu":
        acc = jax.nn.gelu(acc, approximate=True)
        o_ref[...] = acc.astype(o_ref.dtype)
    elif epilogue == "rope":
        cos = rest[0][...]
        sin = rest[1][...]
        lane = lax.broadcasted_iota(jnp.int32, cos.shape, 1)
        first = (lane % 32) < 16
        for cb in range(acc.shape[1] // LANES):
            a = acc[:, cb * LANES:(cb + 1) * LANES]
            rot = jnp.where(first, pltpu.roll(a, LANES - 16, 1), pltpu.roll(a, 16, 1))
            r = (a * cos + rot * sin) * out_scale
            o_ref[:, cb * LANES:(cb + 1) * LANES] = r.astype(o_ref.dtype)
    else:
        o_ref[...] = acc.astype(o_ref.dtype)


def _nm_matmul(tok, layer, xall, g, modf, w, out_dtype, epilogue="none", tables=None, out_scale=1.0):
    ntok, d = xall.shape
    tm = tok.tile
    nw = w.shape[1]
    in_specs = [pl.BlockSpec((tm, d), lambda i: (i, 0)),
                pl.BlockSpec((1, d), lambda i: (0, 0)),
                pl.BlockSpec((1, 1, d), tok.mod_map(layer, 0)),
                pl.BlockSpec((1, 1, d), tok.mod_map(layer, 1)),
                pl.BlockSpec((d, nw), lambda i: (0, 0))]
    args = [xall, g, modf, modf, w]
    if epilogue == "rope":
        tmap = lambda i: (jnp.where(i < tok.nx, i % tok.spt, tok.spt), 0)
        in_specs += [pl.BlockSpec((tm, LANES), tmap), pl.BlockSpec((tm, LANES), tmap)]
        args += list(tables)
    return pl.pallas_call(
        functools.partial(_nm_matmul_kernel, epilogue=epilogue, out_scale=out_scale),
        grid=(ntok // tm,),
        in_specs=in_specs,
        out_specs=pl.BlockSpec((tm, nw), lambda i: (i, 0)),
        out_shape=jax.ShapeDtypeStruct((ntok, nw), out_dtype),
        compiler_params=_cparams(1),
        name="nm_matmul_" + epilogue,
    )(*args)


def _dot_nt(a, b):
    return lax.dot_general(a, b, (((1,), (1,)), ((), ())), preferred_element_type=F32)


def _nm_matmul_t_kernel(x_ref, g_ref, sh_ref, sc_ref, wt_ref, *rest, rope, out_scale):
    o_ref = rest[-1]
    h = _norm_mod(x_ref[...], g_ref[...], sh_ref[0], sc_ref[0])
    acc = _dot_nt(wt_ref[...], h.astype(BF16))
    n_c, nw, tc = o_ref.shape
    if rope:
        cos = rest[0][...]
        sin = rest[1][...]
        for g in range(nw // 32):
            a = acc[g * 32:g * 32 + 16]
            b = acc[g * 32 + 16:g * 32 + 32]
            c = cos[(g % 2) * 16:(g % 2) * 16 + 16]
            s = sin[(g % 2) * 16:(g % 2) * 16 + 16]
            ra = ((a * c - b * s) * out_scale).astype(o_ref.dtype)
            rb = ((b * c + a * s) * out_scale).astype(o_ref.dtype)
            for ci in range(n_c):
                o_ref[ci, g * 32:g * 32 + 16, :] = ra[:, ci * tc:(ci + 1) * tc]
                o_ref[ci, g * 32 + 16:g * 32 + 32, :] = rb[:, ci * tc:(ci + 1) * tc]
    else:
        for ci in range(n_c):
            o_ref[ci] = acc[:, ci * tc:(ci + 1) * tc].astype(o_ref.dtype)


def _nm_matmul_t(tok, layer, xall, g, modf, wt, tc, tables=None, out_scale=1.0):
    ntok, d = xall.shape
    tm = tok.tile
    nw = wt.shape[0]
    in_specs = [pl.BlockSpec((tm, d), lambda i: (i, 0)),
                pl.BlockSpec((1, d), lambda i: (0, 0)),
                pl.BlockSpec((1, 1, d), tok.mod_map(layer, 0)),
                pl.BlockSpec((1, 1, d), tok.mod_map(layer, 1)),
                pl.BlockSpec((nw, d), lambda i: (0, 0))]
    args = [xall, g, modf, modf, wt]
    if tables is not None:
        tmap = lambda i: (0, jnp.where(i < tok.nx, i % tok.spt, tok.spt))
        in_specs += [pl.BlockSpec((32, tm), tmap), pl.BlockSpec((32, tm), tmap)]
        args += list(tables)
    return pl.pallas_call(
        functools.partial(_nm_matmul_t_kernel, rope=tables is not None, out_scale=out_scale),
        grid=(ntok // tm,),
        in_specs=in_specs,
        out_specs=pl.BlockSpec((tm // tc, nw, tc), lambda i: (i, 0, 0)),
        out_shape=jax.ShapeDtypeStruct((ntok // tc, nw, tc), BF16),
        compiler_params=_cparams(1),
        name="nm_matmul_t_rope" if tables is not None else "nm_matmul_t",
    )(*args)


def _softplus(x):
    return jnp.maximum(x, 0.0) + jnp.log(1.0 + jnp.exp(-jnp.abs(x)))


def _rglru_kernel(ux_ref, uc_ref, yx_ref, yc_ref, cw_ref, cb_ref, wg_ref, gb_ref, lam_ref,
                  ox_ref, oc_ref, a_s, b_s, hf_s, hb_s, *, S, C, chunk):
    ka = _softplus(-lam_ref[...]) * (-RG_C * math.log2(math.e))
    cw = cw_ref[...]
    cb = cb_ref[...]
    wg = wg_ref[0]
    gb = gb_ref[0]

    def coeffs(u_ref, T):
        rc = min(chunk, T)

        def body(ci, _):
            c0 = pl.multiple_of(ci * rc, rc)
            cur = u_ref[pl.ds(c0, rc), :]
            pstart = pl.multiple_of(jnp.maximum(c0 - SUBLANES, 0), SUBLANES)
            nstart = pl.multiple_of(jnp.minimum(c0 + rc, T - SUBLANES), SUBLANES)
            prev = jnp.where(c0 > 0, u_ref[pl.ds(pstart, SUBLANES), :], 0.0)
            nxt = jnp.where(c0 + rc < T, u_ref[pl.ds(nstart, SUBLANES), :], 0.0)
            ext = jnp.concatenate([prev, cur, nxt], axis=0)
            n_ext = rc + 2 * SUBLANES
            xm2 = pltpu.roll(ext, 2, 0)[SUBLANES:SUBLANES + rc]
            xm1 = pltpu.roll(ext, 1, 0)[SUBLANES:SUBLANES + rc]
            xp1 = pltpu.roll(ext, n_ext - 1, 0)[SUBLANES:SUBLANES + rc]
            u = xm2 * cw[0:1] + xm1 * cw[1:2] + cur * cw[2:3] + xp1 * cw[3:4] + cb
            z = jnp.dot(u.astype(BF16), wg, preferred_element_type=F32) + gb
            for d in range(2):
                r = jax.nn.sigmoid(z[:, d * 2 * LANES:d * 2 * LANES + LANES])
                ig = jax.nn.sigmoid(z[:, d * 2 * LANES + LANES:(d + 1) * 2 * LANES])
                a = jnp.exp2(r * ka[d:d + 1])
                om = 1.0 - a * a
                root = om * lax.rsqrt(jnp.maximum(om, TINY))
                bcoef = root * (ig * u)
                a_s[d, pl.ds(c0, rc), :] = a
                b_s[d, pl.ds(c0, rc), :] = bcoef
            return 0

        lax.fori_loop(0, T // rc, body, 0)

    row = lax.broadcasted_iota(jnp.int32, (SUBLANES, LANES), 0)

    def scans(T, cf, cbk):
        def body(k, carry):
            cf, cbk = carry
            rf = pl.multiple_of(k * SUBLANES, SUBLANES)
            rb = pl.multiple_of(T - SUBLANES - k * SUBLANES, SUBLANES)
            a = a_s[0, pl.ds(rf, SUBLANES), :]
            b = b_s[0, pl.ds(rf, SUBLANES), :]
            for s in (1, 2, 4):
                m = row >= s
                b = jnp.where(m, a * pltpu.roll(b, s, 0), 0.0) + b
                a = jnp.where(m, a * pltpu.roll(a, s, 0), a)
            h = b + a * cf
            hf_s[pl.ds(rf, SUBLANES), :] = h
            cf = h[SUBLANES - 1:SUBLANES, :]
            a = a_s[1, pl.ds(rb, SUBLANES), :]
            b = b_s[1, pl.ds(rb, SUBLANES), :]
            for s in (1, 2, 4):
                m = row < SUBLANES - s
                b = jnp.where(m, a * pltpu.roll(b, SUBLANES - s, 0), 0.0) + b
                a = jnp.where(m, a * pltpu.roll(a, SUBLANES - s, 0), a)
            h = b + a * cbk
            hb_s[pl.ds(rb, SUBLANES), :] = h
            cbk = h[0:1, :]
            return cf, cbk

        return lax.fori_loop(0, T // SUBLANES, body, (cf, cbk), unroll=SCAN_UNROLL)

    zero = jnp.zeros((1, LANES), F32)
    coeffs(uc_ref, C)
    scans(C, zero, zero)
    oc_ref[...] = ((hf_s[0:C, :] + hb_s[0:C, :]) * yc_ref[...].astype(F32)).astype(oc_ref.dtype)
    h0f = hf_s[C - 1:C, :]
    h0b = hb_s[0:1, :]
    coeffs(ux_ref, S)
    scans(S, h0f, h0b)
    ox_ref[...] = ((hf_s[0:S, :] + hb_s[0:S, :]) * yx_ref[...].astype(F32)).astype(ox_ref.dtype)


def _rglru(B, S, C, u_all, y_all, conv_w, conv_b, wg, gb, lam):
    d_rnn = u_all.shape[1]
    nblk = d_rnn // LANES
    cbase = B * S // C
    tmax = max(S, C)
    return pl.pallas_call(
        functools.partial(_rglru_kernel, S=S, C=C, chunk=SCAN_CHUNK),
        grid=(B, nblk),
        in_specs=[pl.BlockSpec((S, LANES), lambda b, n: (b, n)),
                  pl.BlockSpec((C, LANES), lambda b, n: (cbase + b, n)),
                  pl.BlockSpec((S, LANES), lambda b, n: (b, n)),
                  pl.BlockSpec((C, LANES), lambda b, n: (cbase + b, n)),
                  pl.BlockSpec((CONV_W, LANES), lambda b, n: (0, n)),
                  pl.BlockSpec((1, LANES), lambda b, n: (0, n)),
                  pl.BlockSpec((1, LANES, 4 * LANES), lambda b, n: (n, 0, 0)),
                  pl.BlockSpec((1, 1, 4 * LANES), lambda b, n: (n, 0, 0)),
                  pl.BlockSpec((2, LANES), lambda b, n: (0, n))],
        out_specs=[pl.BlockSpec((S, LANES), lambda b, n: (b, n)),
                   pl.BlockSpec((C, LANES), lambda b, n: (b, n))],
        out_shape=[jax.ShapeDtypeStruct((B * S, d_rnn), BF16),
                   jax.ShapeDtypeStruct((B * C, d_rnn), BF16)],
        scratch_shapes=[pltpu.VMEM((2, tmax, LANES), F32), pltpu.VMEM((2, tmax, LANES), F32),
                        pltpu.VMEM((tmax, LANES), F32), pltpu.VMEM((tmax, LANES), F32)],
        compiler_params=_cparams(2),
        name="rglru",
    )(u_all, u_all, y_all, y_all, conv_w, conv_b, wg, gb, lam)


def _attn_kernel(q_ref, kx_ref, kc_ref, vx_ref, vc_ref, lam_ref, g_ref, o_ref, *s_refs, lambda_init):
    dh2 = g_ref.shape[0]
    n_h = q_ref.shape[1] // dh2
    tq = q_ref.shape[2]
    tk = kx_ref.shape[0] // vx_ref.shape[0]
    n_x = vx_ref.shape[0]
    group = ATTN_KEY_GROUP
    assert n_x % group == 0 and 0 < vc_ref.shape[0] <= group
    n_g = n_x // group
    gk = group * tk
    n_c = vc_ref.shape[0] * tk
    heads = range(n_h)
    hs = [slice(hh * dh2, (hh + 1) * dh2) for hh in heads]

    qz = []
    for hh in heads:
        qt = q_ref[0, hs[hh], :]
        row = lax.broadcasted_iota(jnp.int32, qt.shape, 0)
        zero = jnp.zeros_like(qt)
        qz.append(jnp.concatenate([jnp.where(row < dh2 // 2, qt, zero), jnp.where(row >= dh2 // 2, qt, zero)],
                                  axis=1))

    def scores(hh, kblk):
        return jnp.dot(kblk, qz[hh], preferred_element_type=F32)

    def xkeys(hh, g):
        return kx_ref[g * gk:(g + 1) * gk, hs[hh]]

    def xvals(hh, g):
        return jnp.concatenate([vx_ref[g * group + c, hs[hh], :] for c in range(group)], axis=1)

    def cvals(hh):
        return jnp.concatenate([vc_ref[c, hs[hh], :] for c in range(vc_ref.shape[0])], axis=1)

    lv = lam_ref[...]
    lam = (jnp.exp(jnp.sum(lv[0:1] * lv[1:2], axis=-1, keepdims=True))
           - jnp.exp(jnp.sum(lv[2:3] * lv[3:4], axis=-1, keepdims=True)) + lambda_init)

    def finish(hh, l, acc):
        ot = acc[:, :tq] / l[:, :tq] - lam * (acc[:, tq:] / l[:, tq:])
        ms = jnp.mean(ot * ot, axis=0, keepdims=True)
        ot = (ot * lax.rsqrt(ms + EPS)) * g_ref[...] * (1.0 - lambda_init)
        o_ref[:, hs[hh]] = ot.T.astype(o_ref.dtype)

    bufs = [(s_refs[2 * hh], s_refs[2 * hh + 1]) for hh in heads]
    m, l, acc, top = {}, {}, {}, {}
    sc = [scores(hh, kc_ref[:, hs[hh]]) for hh in heads]
    for hh in heads:
        bufs[hh][0][...] = scores(hh, xkeys(hh, 0))
    for hh in heads:
        m[hh] = jnp.max(sc[hh], axis=0, keepdims=True)
        p = jnp.exp2(sc[hh] - m[hh])
        l[hh] = jnp.sum(p, axis=0, keepdims=True)
        acc[hh] = jnp.dot(cvals(hh), p.astype(BF16), preferred_element_type=F32)
        top[hh] = m[hh]
    for g in range(n_g):
        for hh in heads:
            if g + 1 < n_g:
                bufs[hh][(g + 1) % 2][...] = scores(hh, xkeys(hh, g + 1))
            s = bufs[hh][g % 2][...]
            top[hh] = jnp.maximum(top[hh], jnp.max(s, axis=0, keepdims=True))
            p = jnp.exp2(s - m[hh])
            l[hh] = l[hh] + jnp.sum(p, axis=0, keepdims=True)
            acc[hh] = acc[hh] + jnp.dot(xvals(hh, g), p.astype(BF16), preferred_element_type=F32)
    for hh in heads:
        finish(hh, l[hh], acc[hh])

    def update(st, s, vt):
        mo, lo, ao = st
        mn = jnp.maximum(mo, jnp.max(s, axis=0, keepdims=True))
        alpha = jnp.exp2(mo - mn)
        p = jnp.exp2(s - mn)
        lo = alpha * lo + jnp.sum(p, axis=0, keepdims=True)
        ao = alpha * ao + jnp.dot(vt, p.astype(BF16), preferred_element_type=F32)
        return mn, lo, ao

    for hh in heads:
        excess = jnp.max(top[hh] - m[hh])

        @pl.when(jnp.logical_not(excess <= ATTN_SHIFT_SLACK))
        def _(hh=hh):
            st = (jnp.full((1, 2 * tq), -jnp.inf, F32), jnp.zeros((1, 2 * tq), F32),
                  jnp.zeros((dh2, 2 * tq), F32))
            bh = bufs[hh]
            bh[0][...] = scores(hh, xkeys(hh, 0))
            for g in range(n_g):
                nxt = bh[(g + 1) % 2]
                if g + 1 < n_g:
                    nxt[...] = scores(hh, xkeys(hh, g + 1))
                else:
                    nxt[0:n_c, :] = scores(hh, kc_ref[:, hs[hh]])
                st = update(st, bh[g % 2][...], xvals(hh, g))
            st = update(st, bh[n_g % 2][0:n_c, :], cvals(hh))
            finish(hh, st[1], st[2])


def _attention(B, S, C, heads, qt_all, k_all, vt_all, lam, subln_g, lambda_init):
    d = k_all.shape[1]
    dh2 = d // heads
    tq = qt_all.shape[2]
    tc = vt_all.shape[2]
    qpb = S // tq
    cbase = B * S // C
    hp = ATTN_HEADS_PER_STEP if heads % ATTN_HEADS_PER_STEP == 0 else 1
    hw = hp * dh2
    return pl.pallas_call(
        functools.partial(_attn_kernel, lambda_init=lambda_init),
        grid=(B, heads // hp, qpb),
        in_specs=[pl.BlockSpec((1, hw, tq), lambda b, h, i: (b * qpb + i, h, 0)),
                  pl.BlockSpec((S, hw), lambda b, h, i: (b, h)),
                  pl.BlockSpec((C, hw), lambda b, h, i: (cbase + b, h)),
                  pl.BlockSpec((S // tc, hw, tc), lambda b, h, i: (b, h, 0)),
                  pl.BlockSpec((C // tc, hw, tc), lambda b, h, i: (cbase + b, h, 0)),
                  pl.BlockSpec(lam.shape, lambda b, h, i: (0, 0)),
                  pl.BlockSpec((dh2, 1), lambda b, h, i: (0, 0))],
        out_specs=pl.BlockSpec((tq, hw), lambda b, h, i: (b * qpb + i, h)),
        out_shape=jax.ShapeDtypeStruct((B * S, d), BF16),
        scratch_shapes=[pltpu.VMEM((ATTN_KEY_GROUP * tc, 2 * tq), F32) for _ in range(2 * hp)],
        compiler_params=_cparams(3),
        name="diff_attn",
    )(qt_all, k_all, k_all, vt_all, vt_all, lam, subln_g)


def _proj_router_kernel(*refs, two_sources, nx):
    if two_sources:
        gx_ref, gc_ref = refs[:2]
        refs = refs[2:]
    else:
        gx_ref = refs[0]
        refs = refs[1:]
    (w_ref, x_ref, gate_ref, g2_ref, sh_ref, sc_ref, rwh_ref, rwl_ref, x1_ref, h2_ref, st_ref) = refs
    if two_sources:
        i = pl.program_id(0)
        gin = jnp.where(i < nx, gx_ref[...], gc_ref[...])
    else:
        gin = gx_ref[...]
    o = jnp.dot(gin, w_ref[...], preferred_element_type=F32)
    x1 = x_ref[...] + gate_ref[0] * o
    x1_ref[...] = x1
    h2 = _norm_mod(x1, g2_ref[...], sh_ref[0], sc_ref[0])
    h2_ref[...] = h2
    hi = h2.astype(BF16)
    lo = (h2 - hi.astype(F32)).astype(BF16)
    rwh = rwh_ref[...]
    logits = _dot_nt(rwh, hi) + _dot_nt(rwh, lo) + _dot_nt(rwl_ref[...], hi)
    st_ref[...] = jax.nn.sigmoid(logits)


def _proj_router(tok, layer, n_rows, g_x, g_c, w, xall, norm2_g, modf, rw_hi, rw_lo):
    d = xall.shape[1]
    tm = tok.tile
    e = rw_hi.shape[0]
    two = g_c is not None
    nxm1 = tok.nx - 1
    in_specs = []
    args = []
    if two:
        in_specs += [pl.BlockSpec((tm, d), lambda i: (jnp.minimum(i, nxm1), 0)),
                     pl.BlockSpec((tm, d), lambda i: (jnp.maximum(i - tok.nx, 0), 0))]
        args += [g_x, g_c]
    else:
        in_specs += [pl.BlockSpec((tm, d), lambda i: (i, 0))]
        args += [g_x]
    in_specs += [pl.BlockSpec((d, d), lambda i: (0, 0)),
                 pl.BlockSpec((tm, d), lambda i: (i, 0)),
                 pl.BlockSpec((1, 1, d), tok.mod_map(layer, 2)),
                 pl.BlockSpec((1, d), lambda i: (0, 0)),
                 pl.BlockSpec((1, 1, d), tok.mod_map(layer, 3)),
                 pl.BlockSpec((1, 1, d), tok.mod_map(layer, 4)),
                 pl.BlockSpec((e, d), lambda i: (0, 0)),
                 pl.BlockSpec((e, d), lambda i: (0, 0))]
    args += [w, xall, modf, norm2_g, modf, modf, rw_hi, rw_lo]
    return pl.pallas_call(
        functools.partial(_proj_router_kernel, two_sources=two, nx=tok.nx),
        grid=(n_rows // tm,),
        in_specs=in_specs,
        out_specs=[pl.BlockSpec((tm, d), lambda i: (i, 0)),
                   pl.BlockSpec((tm, d), lambda i: (i, 0)),
                   pl.BlockSpec((e, tm), lambda i: (0, i))],
        out_shape=[jax.ShapeDtypeStruct((n_rows, d), F32),
                   jax.ShapeDtypeStruct((n_rows, d), F32),
                   jax.ShapeDtypeStruct((e, n_rows), F32)],
        compiler_params=_cparams(1),
        name="proj_router",
    )(*args)


def _route_kernel(s_ref, bias_ref, u_ref, ints_ref, wts_ref, cnt_ref, carry_ref, *, n_groups):
    i = pl.program_id(0)

    @pl.when(i == 0)
    def _():
        carry_ref[...] = jnp.zeros_like(carry_ref)

    s = s_ref[...]
    n_e, tr = s.shape
    epg = n_e // n_groups
    sel = s + bias_ref[...]
    ridx = lax.broadcasted_iota(jnp.int32, (epg, tr), 0).astype(F32)
    big = float(epg)
    best = gi = bi1 = bi2 = None
    for g in range(n_groups):
        sg = sel[g * epg:(g + 1) * epg, :]
        m1 = jnp.max(sg, axis=0, keepdims=True)
        i1 = jnp.min(jnp.where(sg == m1, ridx, big), axis=0, keepdims=True)
        sg2 = jnp.where(ridx == i1, -jnp.inf, sg)
        m2 = jnp.max(sg2, axis=0, keepdims=True)
        i2 = jnp.min(jnp.where(sg2 == m2, ridx, big), axis=0, keepdims=True)
        score = m1 + m2
        if g == 0:
            best, gi, bi1, bi2 = score, jnp.zeros_like(i1), i1, i2
        else:
            better = score > best
            best = jnp.where(better, score, best)
            gi = jnp.where(better, float(g), gi)
            bi1 = jnp.where(better, i1, bi1)
            bi2 = jnp.where(better, i2, bi2)
    e0 = gi * epg + bi1
    e1 = gi * epg + bi2
    eidx = lax.broadcasted_iota(jnp.int32, (n_e, tr), 0).astype(F32)
    oh0 = eidx == e0
    oh1 = eidx == e1
    w0 = jnp.sum(jnp.where(oh0, s, 0.0), axis=0, keepdims=True)
    w1 = jnp.sum(jnp.where(oh1, s, 0.0), axis=0, keepdims=True)
    den = w0 + w1
    wts_ref[0:1, :] = w0 / den
    wts_ref[1:2, :] = w1 / den
    f0 = jnp.where(oh0, 1.0, 0.0)
    f1 = jnp.where(oh1, 1.0, 0.0)
    u = u_ref[...]
    p0 = jnp.dot(f0.astype(BF16), u, preferred_element_type=F32)
    p1 = jnp.dot(f1.astype(BF16), u, preferred_element_type=F32)
    carry = carry_ref[...]
    c0 = jnp.sum(f0, axis=1, keepdims=True)
    c1 = jnp.sum(f1, axis=1, keepdims=True)
    rank0 = jnp.sum(f0 * (carry + p0), axis=0, keepdims=True)
    rank1 = jnp.sum(f1 * (carry + c0 + p1), axis=0, keepdims=True)
    ints_ref[0:1, :] = e0.astype(jnp.int32)
    ints_ref[1:2, :] = e1.astype(jnp.int32)
    ints_ref[2:3, :] = rank0.astype(jnp.int32)
    ints_ref[3:4, :] = rank1.astype(jnp.int32)
    carry = carry + c0 + c1
    carry_ref[...] = carry
    cnt_ref[...] = jnp.broadcast_to(carry, cnt_ref.shape)


def _route(scores_t, bias):
    n_e, nt = scores_t.shape
    tr = max(t for t in range(LANES, min(ROUTE_TILE, nt) + 1, LANES) if nt % t == 0)
    tri = jnp.triu(jnp.ones((tr, tr), BF16), k=1)
    return pl.pallas_call(
        functools.partial(_route_kernel, n_groups=N_GROUPS),
        grid=(nt // tr,),
        in_specs=[pl.BlockSpec((n_e, tr), lambda i: (0, i)),
                  pl.BlockSpec((n_e, 1), lambda i: (0, 0)),
                  pl.BlockSpec((tr, tr), lambda i: (0, 0))],
        out_specs=[pl.BlockSpec((4, tr), lambda i: (0, i)),
                   pl.BlockSpec((2, tr), lambda i: (0, i)),
                   pl.BlockSpec((n_e, LANES), lambda i: (0, 0))],
        out_shape=[jax.ShapeDtypeStruct((4, nt), jnp.int32),
                   jax.ShapeDtypeStruct((2, nt), F32),
                   jax.ShapeDtypeStruct((n_e, LANES), F32)],
        scratch_shapes=[pltpu.VMEM((n_e, 1), F32)],
        compiler_params=_cparams(1),
        name="route",
    )(scores_t, bias.reshape(n_e, 1), tri)


def _row_dmas(n_rows, make_copy, sem_wait_copy):
    def start(tb, _):
        for u in range(DMA_UNROLL):
            t = tb * DMA_UNROLL + u
            for k in range(2):
                make_copy(t, k).start(priority=(2 * u + k) % 2)
        return 0

    lax.fori_loop(0, n_rows // DMA_UNROLL, start, 0)

    def wait(tb, _):
        for _u in range(2 * WAIT_UNROLL):
            sem_wait_copy().wait()
        return 0

    lax.fori_loop(0, n_rows // WAIT_UNROLL, wait, 0)


def _dispatch_kernel(fill_ref, dest_ref, h_ref, xs_ref, zeros_ref, sem, fill_sem, *, td, n_e, tb):
    _row_dmas(td,
              lambda t, k: pltpu.make_async_copy(h_ref.at[pl.ds(t, 1), :],
                                                 xs_ref.at[pl.ds(dest_ref[k * td + t], 1), :], sem),
              lambda: pltpu.make_async_copy(h_ref.at[pl.ds(0, 1), :], xs_ref.at[pl.ds(0, 1), :], sem))

    @pl.when(pl.program_id(0) == 0)
    def _():
        zeros_ref[...] = jnp.zeros_like(zeros_ref)
        bits = [1 << b for b in range(tb.bit_length() - 2, 2, -1)]

        def run(phase, pred, n_rows, off):
            cp = pltpu.make_async_copy(zeros_ref.at[pl.ds(0, n_rows), :], xs_ref.at[pl.ds(off, n_rows), :],
                                       fill_sem)

            @pl.when(pred)
            def _():
                if phase == "start":
                    cp.start()
                else:
                    cp.wait()

        half = tb // 2
        n_half = xs_ref.shape[0] // half

        def tail(phase):
            def body(j, _):
                cp = pltpu.make_async_copy(zeros_ref, xs_ref.at[pl.ds(pl.multiple_of(j * half, half), half), :],
                                           fill_sem)
                if phase == "start":
                    cp.start()
                else:
                    cp.wait()
                return 0

            lax.fori_loop(fill_ref[0, n_e], n_half, body, 0)

        for phase in ("start", "wait"):
            for e in range(n_e):
                first, n_single, off, n = (fill_ref[r, e] for r in range(4))
                for r in range(SUBLANES - 1):
                    run(phase, r < n_single, 1, first + r)
                for bit in bits:
                    run(phase, (n & bit) != 0, bit, pl.multiple_of(off, SUBLANES))
                    off = off + (n & bit)
            tail(phase)


def _dispatch(fill, dest, h2, n_slots, n_e):
    nt, d = h2.shape
    td = min(DMA_TILE, nt)
    tb = MOE_TILE
    gs = pltpu.PrefetchScalarGridSpec(
        num_scalar_prefetch=1,
        grid=(nt // td,),
        in_specs=[pl.BlockSpec((2 * td,), lambda i, f: (i,), memory_space=pltpu.SMEM),
                  pl.BlockSpec((td, d), lambda i, f: (i, 0))],
        out_specs=pl.BlockSpec(memory_space=pl.ANY),
        scratch_shapes=[pltpu.VMEM((tb // 2, d), h2.dtype), pltpu.SemaphoreType.DMA(()),
                        pltpu.SemaphoreType.DMA(())],
    )
    return pl.pallas_call(
        functools.partial(_dispatch_kernel, td=td, n_e=n_e, tb=tb),
        grid_spec=gs,
        out_shape=jax.ShapeDtypeStruct((n_slots, d), h2.dtype),
        compiler_params=_cparams(1),
        name="moe_dispatch",
    )(fill, dest, h2)


def _ffn_kernel(be_ref, bi_ref, nu_ref, xs_ref, wg_ref, wu_ref, wd_ref, y_ref, wgb, wub, wdb):
    j = pl.program_id(0)

    @pl.when(j < nu_ref[0])
    def _():
        prev = be_ref[jnp.maximum(j - 1, 0)]
        changed = jnp.logical_or(j == 0, be_ref[j] != prev)

        @pl.when(changed)
        def _():
            wgb[...] = wg_ref[0, 0].astype(BF16)
            wub[...] = wu_ref[0, 0].astype(BF16)
            wdb[...] = wd_ref[0, 0].astype(BF16)

        x = xs_ref[...].astype(BF16)
        g = jnp.dot(x, wgb[...], preferred_element_type=F32)
        u = jnp.dot(x, wub[...], preferred_element_type=F32)
        h = (g * jax.nn.sigmoid(g)) * u
        y_ref[...] = jnp.dot(h.astype(BF16), wdb[...], preferred_element_type=F32)

    @pl.when(j >= nu_ref[0])
    def _():
        y_ref[...] = jnp.zeros_like(y_ref)


def _ffn(layer, blk_e, blk_in, n_used, xs, w_gate, w_up, w_down):
    n_slots, d = xs.shape
    de = w_gate.shape[-1]
    tb = MOE_TILE
    nb = n_slots // tb
    gs = pltpu.PrefetchScalarGridSpec(
        num_scalar_prefetch=3,
        grid=(nb,),
        in_specs=[pl.BlockSpec((tb, d), lambda j, be, bi, nu: (bi[j], 0)),
                  pl.BlockSpec((1, 1, d, de), lambda j, be, bi, nu: (layer, be[j], 0, 0)),
                  pl.BlockSpec((1, 1, d, de), lambda j, be, bi, nu: (layer, be[j], 0, 0)),
                  pl.BlockSpec((1, 1, de, d), lambda j, be, bi, nu: (layer, be[j], 0, 0))],
        out_specs=pl.BlockSpec((tb, d), lambda j, be, bi, nu: (j, 0)),
        scratch_shapes=[pltpu.VMEM((d, de), BF16), pltpu.VMEM((d, de), BF16), pltpu.VMEM((de, d), BF16)],
    )
    return pl.pallas_call(
        _ffn_kernel,
        grid_spec=gs,
        out_shape=jax.ShapeDtypeStruct((n_slots, d), F32),
        compiler_params=_cparams(1),
        name="moe_ffn",
    )(blk_e, blk_in, n_used, xs, w_gate, w_up, w_down)


def _combine_kernel(dest_ref, wts_ref, x1_ref, gate_ref, *rest, td, final):
    if final:
        fg_ref, yb_ref, o_ref, ybuf, sem = rest
    else:
        yb_ref, o_ref, ybuf, sem = rest

    _row_dmas(td,
              lambda t, k: pltpu.make_async_copy(yb_ref.at[pl.ds(dest_ref[k * td + t], 1), :],
                                                 ybuf.at[k, pl.ds(t, 1), :], sem),
              lambda: pltpu.make_async_copy(yb_ref.at[pl.ds(0, 1), :], ybuf.at[0, pl.ds(0, 1), :], sem))
    w = wts_ref[...]
    y = ybuf[0] * w[:, 0:1] + ybuf[1] * w[:, 1:2]
    xn = x1_ref[...] + gate_ref[0] * y
    if final:
        ms = jnp.mean(xn * xn, axis=-1, keepdims=True)
        xn = (xn * lax.rsqrt(ms + EPS)) * fg_ref[...]
    o_ref[...] = xn


def _combine(tok, layer, dest, wts_rows, x1, modf, yb, final_g):
    nt, d = x1.shape
    td = tok.tile
    final = final_g is not None
    in_specs = [pl.BlockSpec((2 * td,), lambda i: (i,), memory_space=pltpu.SMEM),
                pl.BlockSpec((td, 2), lambda i: (i, 0)),
                pl.BlockSpec((td, d), lambda i: (i, 0)),
                pl.BlockSpec((1, 1, d), tok.mod_map(layer, 5))]
    args = [dest, wts_rows, x1, modf]
    if final:
        in_specs.append(pl.BlockSpec((1, d), lambda i: (0, 0)))
        args.append(final_g)
    in_specs.append(pl.BlockSpec(memory_space=pl.ANY))
    args.append(yb)
    return pl.pallas_call(
        functools.partial(_combine_kernel, td=td, final=final),
        grid=(nt // td,),
        in_specs=in_specs,
        out_specs=pl.BlockSpec((td, d), lambda i: (i, 0)),
        out_shape=jax.ShapeDtypeStruct((nt, d), F32),
        scratch_shapes=[pltpu.VMEM((2, td, d), F32), pltpu.SemaphoreType.DMA(())],
        compiler_params=_cparams(1),
        name="moe_combine",
    )(*args)


def _moe(tok_dma, layer, scores_t, h2, x1, modf, router_bias, w_gate, w_up, w_down, final_g):
    nt = h2.shape[0]
    n_e = w_gate.shape[1]
    tb = MOE_TILE
    ints, wts, cnt = _route(scores_t, router_bias.astype(F32))
    counts = cnt[:, 0].astype(jnp.int32)
    padded = ((counts + tb - 1) // tb) * tb
    eidx = jnp.arange(n_e, dtype=jnp.int32)
    p_end = jnp.sum(jnp.where(eidx[None, :] <= eidx[:, None], padded[None, :], 0), axis=1)
    p_start = (p_end - padded).astype(jnp.int32)
    nb = (2 * nt) // tb + n_e
    n_used = (p_end[-1] // tb).astype(jnp.int32)
    jb = jnp.arange(nb, dtype=jnp.int32)
    be = jnp.sum((p_end[None, :] <= (jb * tb)[:, None]).astype(jnp.int32), axis=1)
    be = jnp.minimum(be, n_e - 1)
    last = jnp.maximum(n_used - 1, 0)
    blk_e = jnp.where(jb < n_used, be, be[last])
    blk_in = jnp.minimum(jb, last)
    seg = jnp.sum(jnp.where(ints[0:2, :, None] == eidx, p_start, 0), axis=-1)
    td = tok_dma.tile
    dest = (seg + ints[2:4]).reshape(2, nt // td, td).transpose(1, 0, 2).reshape(-1)
    pad0 = p_start + counts
    pad8 = jnp.minimum(((pad0 + SUBLANES - 1) // SUBLANES) * SUBLANES, p_end)
    fill = jnp.stack([pad0, pad8 - pad0, pad8, p_end - pad8]).astype(jnp.int32)
    fill = jnp.concatenate([fill, jnp.full((4, 1), 2 * n_used, jnp.int32)], axis=1)
    xs = _dispatch(fill, dest, h2, nb * tb, n_e)
    yb = _ffn(layer, blk_e, blk_in, n_used.reshape(1), xs, w_gate, w_up, w_down)
    return _combine(tok_dma, layer, dest, wts.T, x1, modf, yb, final_g)


def _rope_tables(S, tile):
    n_rows = S // GRID_W
    row = jnp.repeat(jnp.arange(n_rows), GRID_W).astype(F32)
    col = jnp.tile(jnp.arange(GRID_W), n_rows).astype(F32)
    n_freq = 16
    inv = 1.0 / (ROPE_THETA ** (jnp.arange(n_freq, dtype=F32) / n_freq))
    ang = jnp.stack([row, col], axis=-1)[:, :, None] * inv
    ang = jnp.broadcast_to(ang[:, :, None, :], (S, 2, 2, n_freq)).reshape(S, 4 * n_freq)
    cos, sin = jnp.cos(ang), jnp.sin(ang)
    pick = jnp.array(list(range(n_freq)) + list(range(2 * n_freq, 3 * n_freq)))
    cos_t = jnp.concatenate([cos[:, pick].T, jnp.ones((2 * n_freq, tile), F32)], axis=1)
    sin_t = jnp.concatenate([sin[:, pick].T, jnp.zeros((2 * n_freq, tile), F32)], axis=1)
    sign = jnp.where((jnp.arange(4 * n_freq) % 32) < 16, -1.0, 1.0).astype(F32)
    sin = sin * sign
    cos = jnp.concatenate([jnp.tile(cos, (1, 2)), jnp.ones((tile, LANES), F32)], axis=0)
    sin = jnp.concatenate([jnp.tile(sin, (1, 2)), jnp.zeros((tile, LANES), F32)], axis=0)
    return (cos, sin), (cos_t, sin_t)


def kernel(x, c, ctx, c_ctx, w_mod, b_mod, norm1_g, norm2_g, rg_w_in, rg_conv_w, rg_conv_b,
           rg_gate_a_w, rg_gate_a_b, rg_gate_x_w, rg_gate_x_b, rg_lambda, rg_w_out,
           da_w_qkv, da_lambda, da_subln_g, da_w_o, router_w, router_bias,
           moe_w_gate, moe_w_up, moe_w_down, final_g):
    B, S, D = x.shape
    C = ctx.shape[1]
    depth = w_mod.shape[0]
    assert B + 1 <= COND_PAD and D % LANES == 0
    assert all(l % N_MIXERS == 0 or l == depth - 1 for l in range(depth))
    tok = _Tok(B, S, C, min(ROW_TILE, S, B * C))
    tok_dma = _Tok(B, S, C, min(DMA_TILE, S, B * C))

    cond = jnp.zeros((COND_PAD, D), F32).at[:B].set(c).at[B].set(c_ctx)
    mods = _adaln(cond, w_mod, b_mod)
    modf = mods.reshape(depth * COND_PAD * 6, 1, D)
    xall = jnp.concatenate([x.reshape(B * S, D), ctx.reshape(B * C, D)], axis=0)
    rw_t = router_w.T.astype(F32)
    rw_hi = rw_t.astype(BF16)
    rw_lo = (rw_t - rw_hi.astype(F32)).astype(BF16)

    i_rg = 0
    i_da = 0
    out = None
    for layer in range(depth):
        last = layer == depth - 1
        n1 = norm1_g[layer].reshape(1, D)
        n2 = norm2_g[layer].reshape(1, D)
        if layer % N_MIXERS == 0:
            d_rnn = rg_w_in.shape[2] // 2
            w_in = rg_w_in[i_rg].astype(BF16)
            y_all = _nm_matmul(tok, layer, xall, n1, modf, w_in[:, :d_rnn], BF16, epilogue="gelu")
            u_all = _nm_matmul(tok, layer, xall, n1, modf, w_in[:, d_rnn:], F32)
            nblk = rg_gate_a_w.shape[2]
            wg = jnp.concatenate([rg_gate_a_w[i_rg, 0], rg_gate_x_w[i_rg, 0],
                                  rg_gate_a_w[i_rg, 1], rg_gate_x_w[i_rg, 1]], axis=-1).astype(BF16)
            bsplit = lambda v: v.reshape(nblk, 1, LANES)
            gb = jnp.concatenate([bsplit(rg_gate_a_b[i_rg, 0]), bsplit(rg_gate_x_b[i_rg, 0]),
                                  bsplit(rg_gate_a_b[i_rg, 1]), bsplit(rg_gate_x_b[i_rg, 1])], axis=-1)
            g_x, g_c = _rglru(B, S, C, u_all, y_all, rg_conv_w[i_rg], rg_conv_b[i_rg].reshape(1, d_rnn),
                              wg, gb, rg_lambda[i_rg])
            w_o = rg_w_out[i_rg].astype(BF16)
            i_rg += 1
        else:
            dh = da_lambda.shape[2]
            heads = D // (2 * dh)
            lambda_init = 0.8 - 0.6 * math.exp(-0.3 * layer)
            wqkv = da_w_qkv[i_da].astype(BF16)
            tables, tables_t = _rope_tables(S, tok.tile)
            tc = min(ATTN_KEY_TILE, S, C)
            tq = min(ATTN_Q_TILE, S, tok.tile)
            assert S % tc == 0 and C % tc == 0 and tok.tile % tc == 0 and S % tq == 0 and tok.tile % tq == 0
            qt_all = _nm_matmul_t(tok, layer, xall, n1, modf, wqkv[:, :D].T, tq, tables=tables_t,
                                  out_scale=math.log2(math.e) * dh ** -0.5)
            k_all = _nm_matmul(tok, layer, xall, n1, modf, wqkv[:, D:2 * D], BF16, epilogue="rope",
                               tables=tables)
            vt_all = _nm_matmul_t(tok, layer, xall, n1, modf, wqkv[:, 2 * D:].T, tc)
            g_x = _attention(B, S, C, heads, qt_all, k_all, vt_all, da_lambda[i_da],
                             da_subln_g[i_da].reshape(2 * dh, 1), lambda_init)
            g_c = None
            w_o = da_w_o[i_da].astype(BF16)
            i_da += 1
        if last:
            n_rows = B * S
            g_c = None
        else:
            n_rows = B * S + B * C
        x1, h2, scores_t = _proj_router(tok, layer, n_rows, g_x, g_c, w_o, xall, n2, modf, rw_hi, rw_lo)
        res = _moe(tok_dma, layer, scores_t, h2, x1, modf, router_bias,
                   moe_w_gate, moe_w_up, moe_w_down,
                   final_g.reshape(1, D) if last else None)
        if last:
            out = res.reshape(B, S, D)
        else:
            xall = res
    return out
```

```python
import functools
import math

import jax
import jax.numpy as jnp
from jax import lax
from jax.experimental import pallas as pl
from jax.experimental.pallas import tpu as pltpu

F32 = jnp.float32
BF16 = jnp.bfloat16

EPS = 1e-6
TINY = 1e-30
GRID_W = 64
CONV_W = 4
CONV_LEFT = 2
RG_C = 8.0
ROPE_THETA = 10000.0
N_GROUPS = 4
N_MIXERS = 2

LANES = 128
SUBLANES = 8
VMEM_LIMIT = 56 * 1024 * 1024

ROW_TILE = 512
MOE_TILE = 256
DMA_TILE = 512
DMA_UNROLL = 8
WAIT_UNROLL = 32
ROUTE_TILE = 1024
ATTN_Q_TILE = 256
ATTN_KEY_TILE = 256
ATTN_KEY_GROUP = 1
ATTN_HEADS_PER_STEP = 2
ATTN_SHIFT_SLACK = 64.0
SCAN_CHUNK = 256
SCAN_UNROLL = 8
COND_PAD = 8


def _cparams(n_axes):
    return pltpu.CompilerParams(dimension_semantics=("arbitrary",) * n_axes,
                                vmem_limit_bytes=VMEM_LIMIT)


def _adaln_kernel(cond_ref, w_ref, b_ref, o_ref):
    c = cond_ref[...]
    h = c * jax.nn.sigmoid(c)
    o_ref[0] = jnp.dot(h, w_ref[0], preferred_element_type=F32,
                       precision=lax.Precision.HIGHEST) + b_ref[0]


def _adaln(cond, w_mod, b_mod):
    depth, d, n6 = w_mod.shape
    tn = n6 // 6
    return pl.pallas_call(
        _adaln_kernel,
        grid=(depth, n6 // tn),
        in_specs=[pl.BlockSpec((COND_PAD, d), lambda l, j: (0, 0)),
                  pl.BlockSpec((1, d, tn), lambda l, j: (l, 0, j)),
                  pl.BlockSpec((1, 1, tn), lambda l, j: (l, 0, j))],
        out_specs=pl.BlockSpec((1, COND_PAD, tn), lambda l, j: (l, 0, j)),
        out_shape=jax.ShapeDtypeStruct((depth, COND_PAD, n6), F32),
        compiler_params=_cparams(2),
        name="adaln",
    )(cond, w_mod, b_mod.reshape(depth, 1, n6))


def _norm_mod(x, g, shift, scale):
    ms = jnp.mean(x * x, axis=-1, keepdims=True)
    h = (x * lax.rsqrt(ms + EPS)) * g
    return h * (1.0 + scale) + shift


class _Tok:
    def __init__(self, B, S, C, tile):
        assert S % tile == 0 and (B * C) % tile == 0
        self.B, self.S, self.C, self.tile = B, S, C, tile
        self.nx = B * S // tile
        self.nc = B * C // tile
        self.spt = S // tile

    def cond(self, i):
        return jnp.where(i < self.nx, i // self.spt, self.B)

    def mod_map(self, layer, chunk):
        return lambda i, *_: ((layer * COND_PAD + self.cond(i)) * 6 + chunk, 0, 0)


def _token_specs(tok, srcs):
    tm, d = tok.tile, srcs[0].shape[1]
    if len(srcs) == 1:
        return [pl.BlockSpec((tm, d), lambda i, *_: (i, 0))]
    return [pl.BlockSpec((tm, d), lambda i, *_: (jnp.minimum(i, tok.nx - 1), 0)),
            pl.BlockSpec((tm, d), lambda i, *_: (jnp.maximum(i - tok.nx, 0), 0))]


def _token_tile(x_refs, nx):
    if len(x_refs) == 1:
        return x_refs[0][...]
    return jnp.where(pl.program_id(0) < nx, x_refs[0][...], x_refs[1][...])


def _dot_nt(a, b):
    return lax.dot_general(a, b, (((1,), (1,)), ((), ())), preferred_element_type=F32)


def _rg_in_kernel(*refs, n_src, nx):
    x_refs, (g_ref, sh_ref, sc_ref, wy_ref, wu_ref, y_ref, u_ref) = refs[:n_src], refs[n_src:]
    h = _norm_mod(_token_tile(x_refs, nx), g_ref[...], sh_ref[0], sc_ref[0]).astype(BF16)
    y = jnp.dot(h, wy_ref[...], preferred_element_type=F32)
    y_ref[...] = jax.nn.gelu(y, approximate=True).astype(y_ref.dtype)
    u_ref[...] = jnp.dot(h, wu_ref[...], preferred_element_type=F32)


def _rg_in(tok, layer, srcs, g, modf, w_y, w_u):
    d = srcs[0].shape[1]
    tm = tok.tile
    ntok = (tok.nx + tok.nc) * tm
    nw = w_y.shape[1]
    return pl.pallas_call(
        functools.partial(_rg_in_kernel, n_src=len(srcs), nx=tok.nx),
        grid=(ntok // tm,),
        in_specs=_token_specs(tok, srcs) + [
            pl.BlockSpec((1, d), lambda i: (0, 0)),
            pl.BlockSpec((1, 1, d), tok.mod_map(layer, 0)),
            pl.BlockSpec((1, 1, d), tok.mod_map(layer, 1)),
            pl.BlockSpec((d, nw), lambda i: (0, 0)),
            pl.BlockSpec((d, nw), lambda i: (0, 0))],
        out_specs=[pl.BlockSpec((tm, nw), lambda i: (i, 0)), pl.BlockSpec((tm, nw), lambda i: (i, 0))],
        out_shape=[jax.ShapeDtypeStruct((ntok, nw), BF16), jax.ShapeDtypeStruct((ntok, nw), F32)],
        compiler_params=_cparams(1),
        name="rg_in_proj",
    )(*srcs, g, modf, modf, w_y, w_u)


def _qkv_kernel(x_ref, g_ref, sh_ref, sc_ref, wqt_ref, wk_ref, wvt_ref, cos_ref, sin_ref, cost_ref, sint_ref,
                qt_ref, k_ref, vt_ref, *, q_scale):
    h = _norm_mod(x_ref[...], g_ref[...], sh_ref[0], sc_ref[0]).astype(BF16)

    qt = _dot_nt(wqt_ref[...], h)
    cos = cost_ref[...]
    sin = sint_ref[...]
    n_q, nw, tq = qt_ref.shape
    for g in range(nw // 32):
        a = qt[g * 32:g * 32 + 16]
        b = qt[g * 32 + 16:g * 32 + 32]
        c = cos[(g % 2) * 16:(g % 2) * 16 + 16]
        s = sin[(g % 2) * 16:(g % 2) * 16 + 16]
        ra = ((a * c - b * s) * q_scale).astype(qt_ref.dtype)
        rb = ((b * c + a * s) * q_scale).astype(qt_ref.dtype)
        for ci in range(n_q):
            qt_ref[ci, g * 32:g * 32 + 16, :] = ra[:, ci * tq:(ci + 1) * tq]
            qt_ref[ci, g * 32 + 16:g * 32 + 32, :] = rb[:, ci * tq:(ci + 1) * tq]

    k = jnp.dot(h, wk_ref[...], preferred_element_type=F32)
    cos = cos_ref[...]
    sin = sin_ref[...]
    lane = lax.broadcasted_iota(jnp.int32, cos.shape, 1)
    first = (lane % 32) < 16
    for cb in range(k.shape[1] // LANES):
        a = k[:, cb * LANES:(cb + 1) * LANES]
        rot = jnp.where(first, pltpu.roll(a, LANES - 16, 1), pltpu.roll(a, 16, 1))
        k_ref[:, cb * LANES:(cb + 1) * LANES] = (a * cos + rot * sin).astype(k_ref.dtype)

    vt = _dot_nt(wvt_ref[...], h)
    n_v, _, tc = vt_ref.shape
    for ci in range(n_v):
        vt_ref[ci] = vt[:, ci * tc:(ci + 1) * tc].astype(vt_ref.dtype)


def _qkv(tok, layer, xall, g, modf, wqt, wk, wvt, tables, tables_t, tq, tc, q_scale):
    ntok, d = xall.shape
    tm = tok.tile
    tmap = lambda i: (jnp.where(i < tok.nx, i % tok.spt, tok.spt), 0)
    tmap_t = lambda i: (0, jnp.where(i < tok.nx, i % tok.spt, tok.spt))
    wspec = pl.BlockSpec((d, d), lambda i: (0, 0))
    return pl.pallas_call(
        functools.partial(_qkv_kernel, q_scale=q_scale),
        grid=(ntok // tm,),
        in_specs=[pl.BlockSpec((tm, d), lambda i: (i, 0)),
                  pl.BlockSpec((1, d), lambda i: (0, 0)),
                  pl.BlockSpec((1, 1, d), tok.mod_map(layer, 0)),
                  pl.BlockSpec((1, 1, d), tok.mod_map(layer, 1)),
                  wspec, wspec, wspec,
                  pl.BlockSpec((tm, LANES), tmap), pl.BlockSpec((tm, LANES), tmap),
                  pl.BlockSpec((32, tm), tmap_t), pl.BlockSpec((32, tm), tmap_t)],
        out_specs=[pl.BlockSpec((tm // tq, d, tq), lambda i: (i, 0, 0)),
                   pl.BlockSpec((tm, d), lambda i: (i, 0)),
                   pl.BlockSpec((tm // tc, d, tc), lambda i: (i, 0, 0))],
        out_shape=[jax.ShapeDtypeStruct((ntok // tq, d, tq), BF16),
                   jax.ShapeDtypeStruct((ntok, d), BF16),
                   jax.ShapeDtypeStruct((ntok // tc, d, tc), BF16)],
        compiler_params=_cparams(1),
        name="qkv_proj",
    )(xall, g, modf, modf, wqt, wk, wvt, *tables, *tables_t)


def _softplus(x):
    return jnp.maximum(x, 0.0) + jnp.log(1.0 + jnp.exp(-jnp.abs(x)))


def _rglru_kernel(ux_ref, uc_ref, yx_ref, yc_ref, cw_ref, cb_ref, wg_ref, gb_ref, lam_ref,
                  ox_ref, oc_ref, a_s, b_s, hf_s, hb_s, *, S, C, chunk):
    ka = _softplus(-lam_ref[...]) * (-RG_C * math.log2(math.e))
    cw = cw_ref[...]
    cb = cb_ref[...]
    wg = wg_ref[0]
    gb = gb_ref[0]

    def coeffs(u_ref, T):
        rc = min(chunk, T)

        def body(ci, _):
            c0 = pl.multiple_of(ci * rc, rc)
            cur = u_ref[pl.ds(c0, rc), :]
            pstart = pl.multiple_of(jnp.maximum(c0 - SUBLANES, 0), SUBLANES)
            nstart = pl.multiple_of(jnp.minimum(c0 + rc, T - SUBLANES), SUBLANES)
            prev = jnp.where(c0 > 0, u_ref[pl.ds(pstart, SUBLANES), :], 0.0)
            nxt = jnp.where(c0 + rc < T, u_ref[pl.ds(nstart, SUBLANES), :], 0.0)
            ext = jnp.concatenate([prev, cur, nxt], axis=0)
            n_ext = rc + 2 * SUBLANES
            xm2 = pltpu.roll(ext, 2, 0)[SUBLANES:SUBLANES + rc]
            xm1 = pltpu.roll(ext, 1, 0)[SUBLANES:SUBLANES + rc]
            xp1 = pltpu.roll(ext, n_ext - 1, 0)[SUBLANES:SUBLANES + rc]
            u = xm2 * cw[0:1] + xm1 * cw[1:2] + cur * cw[2:3] + xp1 * cw[3:4] + cb
            z = jnp.dot(u.astype(BF16), wg, preferred_element_type=F32) + gb
            for d in range(2):
                r = jax.nn.sigmoid(z[:, d * 2 * LANES:d * 2 * LANES + LANES])
                ig = jax.nn.sigmoid(z[:, d * 2 * LANES + LANES:(d + 1) * 2 * LANES])
                a = jnp.exp2(r * ka[d:d + 1])
                om = 1.0 - a * a
                root = om * lax.rsqrt(jnp.maximum(om, TINY))
                bcoef = root * (ig * u)
                a_s[d, pl.ds(c0, rc), :] = a
                b_s[d, pl.ds(c0, rc), :] = bcoef
            return 0

        lax.fori_loop(0, T // rc, body, 0)

    row = lax.broadcasted_iota(jnp.int32, (SUBLANES, LANES), 0)

    def scans(T, cf, cbk):
        def body(k, carry):
            cf, cbk = carry
            rf = pl.multiple_of(k * SUBLANES, SUBLANES)
            rb = pl.multiple_of(T - SUBLANES - k * SUBLANES, SUBLANES)
            a = a_s[0, pl.ds(rf, SUBLANES), :]
            b = b_s[0, pl.ds(rf, SUBLANES), :]
            for s in (1, 2, 4):
                m = row >= s
                b = jnp.where(m, a * pltpu.roll(b, s, 0), 0.0) + b
                a = jnp.where(m, a * pltpu.roll(a, s, 0), a)
            h = b + a * cf
            hf_s[pl.ds(rf, SUBLANES), :] = h
            cf = h[SUBLANES - 1:SUBLANES, :]
            a = a_s[1, pl.ds(rb, SUBLANES), :]
            b = b_s[1, pl.ds(rb, SUBLANES), :]
            for s in (1, 2, 4):
                m = row < SUBLANES - s
                b = jnp.where(m, a * pltpu.roll(b, SUBLANES - s, 0), 0.0) + b
                a = jnp.where(m, a * pltpu.roll(a, SUBLANES - s, 0), a)
            h = b + a * cbk
            hb_s[pl.ds(rb, SUBLANES), :] = h
            cbk = h[0:1, :]
            return cf, cbk

        return lax.fori_loop(0, T // SUBLANES, body, (cf, cbk), unroll=SCAN_UNROLL)

    zero = jnp.zeros((1, LANES), F32)
    coeffs(uc_ref, C)
    scans(C, zero, zero)
    oc_ref[...] = ((hf_s[0:C, :] + hb_s[0:C, :]) * yc_ref[...].astype(F32)).astype(oc_ref.dtype)
    h0f = hf_s[C - 1:C, :]
    h0b = hb_s[0:1, :]
    coeffs(ux_ref, S)
    scans(S, h0f, h0b)
    ox_ref[...] = ((hf_s[0:S, :] + hb_s[0:S, :]) * yx_ref[...].astype(F32)).astype(ox_ref.dtype)


def _rglru(B, S, C, u_all, y_all, conv_w, conv_b, wg, gb, lam):
    d_rnn = u_all.shape[1]
    nblk = d_rnn // LANES
    cbase = B * S // C
    tmax = max(S, C)
    return pl.pallas_call(
        functools.partial(_rglru_kernel, S=S, C=C, chunk=SCAN_CHUNK),
        grid=(B, nblk),
        in_specs=[pl.BlockSpec((S, LANES), lambda b, n: (b, n)),
                  pl.BlockSpec((C, LANES), lambda b, n: (cbase + b, n)),
                  pl.BlockSpec((S, LANES), lambda b, n: (b, n)),
                  pl.BlockSpec((C, LANES), lambda b, n: (cbase + b, n)),
                  pl.BlockSpec((CONV_W, LANES), lambda b, n: (0, n)),
                  pl.BlockSpec((1, LANES), lambda b, n: (0, n)),
                  pl.BlockSpec((1, LANES, 4 * LANES), lambda b, n: (n, 0, 0)),
                  pl.BlockSpec((1, 1, 4 * LANES), lambda b, n: (n, 0, 0)),
                  pl.BlockSpec((2, LANES), lambda b, n: (0, n))],
        out_specs=[pl.BlockSpec((S, LANES), lambda b, n: (b, n)),
                   pl.BlockSpec((C, LANES), lambda b, n: (b, n))],
        out_shape=[jax.ShapeDtypeStruct((B * S, d_rnn), BF16),
                   jax.ShapeDtypeStruct((B * C, d_rnn), BF16)],
        scratch_shapes=[pltpu.VMEM((2, tmax, LANES), F32), pltpu.VMEM((2, tmax, LANES), F32),
                        pltpu.VMEM((tmax, LANES), F32), pltpu.VMEM((tmax, LANES), F32)],
        compiler_params=_cparams(2),
        name="rglru",
    )(u_all, u_all, y_all, y_all, conv_w, conv_b, wg, gb, lam)


def _attn_kernel(q_ref, kx_ref, kc_ref, vx_ref, vc_ref, lam_ref, g_ref, o_ref, *s_refs, lambda_init):
    dh2 = g_ref.shape[0]
    n_h = q_ref.shape[1] // dh2
    tq = q_ref.shape[2]
    tk = kx_ref.shape[0] // vx_ref.shape[0]
    n_x = vx_ref.shape[0]
    group = ATTN_KEY_GROUP
    assert n_x % group == 0 and 0 < vc_ref.shape[0] <= group
    n_g = n_x // group
    gk = group * tk
    n_c = vc_ref.shape[0] * tk
    heads = range(n_h)
    hs = [slice(hh * dh2, (hh + 1) * dh2) for hh in heads]

    qz = []
    for hh in heads:
        qt = q_ref[0, hs[hh], :]
        row = lax.broadcasted_iota(jnp.int32, qt.shape, 0)
        zero = jnp.zeros_like(qt)
        qz.append(jnp.concatenate([jnp.where(row < dh2 // 2, qt, zero), jnp.where(row >= dh2 // 2, qt, zero)],
                                  axis=1))

    def scores(hh, kblk):
        return jnp.dot(kblk, qz[hh], preferred_element_type=F32)

    def xkeys(hh, g):
        return kx_ref[g * gk:(g + 1) * gk, hs[hh]]

    def xvals(hh, g):
        return jnp.concatenate([vx_ref[g * group + c, hs[hh], :] for c in range(group)], axis=1)

    def cvals(hh):
        return jnp.concatenate([vc_ref[c, hs[hh], :] for c in range(vc_ref.shape[0])], axis=1)

    lv = lam_ref[...]
    lam = (jnp.exp(jnp.sum(lv[0:1] * lv[1:2], axis=-1, keepdims=True))
           - jnp.exp(jnp.sum(lv[2:3] * lv[3:4], axis=-1, keepdims=True)) + lambda_init)

    def finish(hh, l, acc):
        ot = acc[:, :tq] / l[:, :tq] - lam * (acc[:, tq:] / l[:, tq:])
        ms = jnp.mean(ot * ot, axis=0, keepdims=True)
        ot = (ot * lax.rsqrt(ms + EPS)) * g_ref[...] * (1.0 - lambda_init)
        o_ref[:, hs[hh]] = ot.T.astype(o_ref.dtype)

    bufs = [(s_refs[2 * hh], s_refs[2 * hh + 1]) for hh in heads]
    m, l, acc, top = {}, {}, {}, {}
    sc = [scores(hh, kc_ref[:, hs[hh]]) for hh in heads]
    for hh in heads:
        bufs[hh][0][...] = scores(hh, xkeys(hh, 0))
    for hh in heads:
        m[hh] = jnp.max(sc[hh], axis=0, keepdims=True)
        p = jnp.exp2(sc[hh] - m[hh])
        l[hh] = jnp.sum(p, axis=0, keepdims=True)
        acc[hh] = jnp.dot(cvals(hh), p.astype(BF16), preferred_element_type=F32)
        top[hh] = m[hh]
    for g in range(n_g):
        for hh in heads:
            if g + 1 < n_g:
                bufs[hh][(g + 1) % 2][...] = scores(hh, xkeys(hh, g + 1))
            s = bufs[hh][g % 2][...]
            top[hh] = jnp.maximum(top[hh], jnp.max(s, axis=0, keepdims=True))
            p = jnp.exp2(s - m[hh])
            l[hh] = l[hh] + jnp.sum(p, axis=0, keepdims=True)
            acc[hh] = acc[hh] + jnp.dot(xvals(hh, g), p.astype(BF16), preferred_element_type=F32)
    for hh in heads:
        finish(hh, l[hh], acc[hh])

    def update(st, s, vt):
        mo, lo, ao = st
        mn = jnp.maximum(mo, jnp.max(s, axis=0, keepdims=True))
        alpha = jnp.exp2(mo - mn)
        p = jnp.exp2(s - mn)
        lo = alpha * lo + jnp.sum(p, axis=0, keepdims=True)
        ao = alpha * ao + jnp.dot(vt, p.astype(BF16), preferred_element_type=F32)
        return mn, lo, ao

    for hh in heads:
        excess = jnp.max(top[hh] - m[hh])

        @pl.when(jnp.logical_not(excess <= ATTN_SHIFT_SLACK))
        def _(hh=hh):
            st = (jnp.full((1, 2 * tq), -jnp.inf, F32), jnp.zeros((1, 2 * tq), F32),
                  jnp.zeros((dh2, 2 * tq), F32))
            bh = bufs[hh]
            bh[0][...] = scores(hh, xkeys(hh, 0))
            for g in range(n_g):
                nxt = bh[(g + 1) % 2]
                if g + 1 < n_g:
                    nxt[...] = scores(hh, xkeys(hh, g + 1))
                else:
                    nxt[0:n_c, :] = scores(hh, kc_ref[:, hs[hh]])
                st = update(st, bh[g % 2][...], xvals(hh, g))
            st = update(st, bh[n_g % 2][0:n_c, :], cvals(hh))
            finish(hh, st[1], st[2])


def _attention(B, S, C, heads, qt_all, k_all, vt_all, lam, subln_g, lambda_init):
    d = k_all.shape[1]
    dh2 = d // heads
    tq = qt_all.shape[2]
    tc = vt_all.shape[2]
    qpb = S // tq
    cbase = B * S // C
    hp = ATTN_HEADS_PER_STEP if heads % ATTN_HEADS_PER_STEP == 0 else 1
    hw = hp * dh2
    return pl.pallas_call(
        functools.partial(_attn_kernel, lambda_init=lambda_init),
        grid=(B, heads // hp, qpb),
        in_specs=[pl.BlockSpec((1, hw, tq), lambda b, h, i: (b * qpb + i, h, 0)),
                  pl.BlockSpec((S, hw), lambda b, h, i: (b, h)),
                  pl.BlockSpec((C, hw), lambda b, h, i: (cbase + b, h)),
                  pl.BlockSpec((S // tc, hw, tc), lambda b, h, i: (b, h, 0)),
                  pl.BlockSpec((C // tc, hw, tc), lambda b, h, i: (cbase + b, h, 0)),
                  pl.BlockSpec(lam.shape, lambda b, h, i: (0, 0)),
                  pl.BlockSpec((dh2, 1), lambda b, h, i: (0, 0))],
        out_specs=pl.BlockSpec((tq, hw), lambda b, h, i: (b * qpb + i, h)),
        out_shape=jax.ShapeDtypeStruct((B * S, d), BF16),
        scratch_shapes=[pltpu.VMEM((ATTN_KEY_GROUP * tc, 2 * tq), F32) for _ in range(2 * hp)],
        compiler_params=_cparams(3),
        name="diff_attn",
    )(qt_all, k_all, k_all, vt_all, vt_all, lam, subln_g)


def _proj_router_kernel(*refs, n_g, n_x, nx):
    g_refs, x_refs, refs = refs[:n_g], refs[n_g:n_g + n_x], refs[n_g + n_x:]
    (w_ref, gate_ref, g2_ref, sh_ref, sc_ref, rwh_ref, rwl_ref, x1_ref, h2_ref, st_ref) = refs
    o = jnp.dot(_token_tile(g_refs, nx), w_ref[...], preferred_element_type=F32)
    x1 = _token_tile(x_refs, nx) + gate_ref[0] * o
    x1_ref[...] = x1
    h2 = _norm_mod(x1, g2_ref[...], sh_ref[0], sc_ref[0])
    h2_ref[...] = h2
    hi = h2.astype(BF16)
    lo = (h2 - hi.astype(F32)).astype(BF16)
    rwh = rwh_ref[...]
    logits = _dot_nt(rwh, hi) + _dot_nt(rwh, lo) + _dot_nt(rwl_ref[...], hi)
    st_ref[...] = jax.nn.sigmoid(logits)


def _proj_router(tok, layer, n_rows, g_srcs, w, x_srcs, norm2_g, modf, rw_hi, rw_lo):
    d = w.shape[1]
    tm = tok.tile
    e = rw_hi.shape[0]
    in_specs = _token_specs(tok, g_srcs) + _token_specs(tok, x_srcs)
    in_specs += [pl.BlockSpec((d, d), lambda i: (0, 0)),
                 pl.BlockSpec((1, 1, d), tok.mod_map(layer, 2)),
                 pl.BlockSpec((1, d), lambda i: (0, 0)),
                 pl.BlockSpec((1, 1, d), tok.mod_map(layer, 3)),
                 pl.BlockSpec((1, 1, d), tok.mod_map(layer, 4)),
                 pl.BlockSpec((e, d), lambda i: (0, 0)),
                 pl.BlockSpec((e, d), lambda i: (0, 0))]
    args = [*g_srcs, *x_srcs, w, modf, norm2_g, modf, modf, rw_hi, rw_lo]
    return pl.pallas_call(
        functools.partial(_proj_router_kernel, n_g=len(g_srcs), n_x=len(x_srcs), nx=tok.nx),
        grid=(n_rows // tm,),
        in_specs=in_specs,
        out_specs=[pl.BlockSpec((tm, d), lambda i: (i, 0)),
                   pl.BlockSpec((tm, d), lambda i: (i, 0)),
                   pl.BlockSpec((e, tm), lambda i: (0, i))],
        out_shape=[jax.ShapeDtypeStruct((n_rows, d), F32),
                   jax.ShapeDtypeStruct((n_rows, d), F32),
                   jax.ShapeDtypeStruct((e, n_rows), F32)],
        compiler_params=_cparams(1),
        name="proj_router",
    )(*args)


def _route_kernel(s_ref, bias_ref, u_ref, ints_ref, wts_ref, cnt_ref, carry_ref, *, n_groups):
    i = pl.program_id(0)

    @pl.when(i == 0)
    def _():
        carry_ref[...] = jnp.zeros_like(carry_ref)

    s = s_ref[...]
    n_e, tr = s.shape
    epg = n_e // n_groups
    sel = s + bias_ref[...]
    ridx = lax.broadcasted_iota(jnp.int32, (epg, tr), 0).astype(F32)
    big = float(epg)
    best = gi = bi1 = bi2 = None
    for g in range(n_groups):
        sg = sel[g * epg:(g + 1) * epg, :]
        m1 = jnp.max(sg, axis=0, keepdims=True)
        i1 = jnp.min(jnp.where(sg == m1, ridx, big), axis=0, keepdims=True)
        sg2 = jnp.where(ridx == i1, -jnp.inf, sg)
        m2 = jnp.max(sg2, axis=0, keepdims=True)
        i2 = jnp.min(jnp.where(sg2 == m2, ridx, big), axis=0, keepdims=True)
        score = m1 + m2
        if g == 0:
            best, gi, bi1, bi2 = score, jnp.zeros_like(i1), i1, i2
        else:
            better = score > best
            best = jnp.where(better, score, best)
            gi = jnp.where(better, float(g), gi)
            bi1 = jnp.where(better, i1, bi1)
            bi2 = jnp.where(better, i2, bi2)
    e0 = gi * epg + bi1
    e1 = gi * epg + bi2
    eidx = lax.broadcasted_iota(jnp.int32, (n_e, tr), 0).astype(F32)
    oh0 = eidx == e0
    oh1 = eidx == e1
    w0 = jnp.sum(jnp.where(oh0, s, 0.0), axis=0, keepdims=True)
    w1 = jnp.sum(jnp.where(oh1, s, 0.0), axis=0, keepdims=True)
    den = w0 + w1
    wts_ref[0:1, :] = w0 / den
    wts_ref[1:2, :] = w1 / den
    f0 = jnp.where(oh0, 1.0, 0.0)
    f1 = jnp.where(oh1, 1.0, 0.0)
    u = u_ref[...]
    p0 = jnp.dot(f0.astype(BF16), u, preferred_element_type=F32)
    p1 = jnp.dot(f1.astype(BF16), u, preferred_element_type=F32)
    carry = carry_ref[...]
    c0 = jnp.sum(f0, axis=1, keepdims=True)
    c1 = jnp.sum(f1, axis=1, keepdims=True)
    rank0 = jnp.sum(f0 * (carry + p0), axis=0, keepdims=True)
    rank1 = jnp.sum(f1 * (carry + c0 + p1), axis=0, keepdims=True)
    ints_ref[0:1, :] = e0.astype(jnp.int32)
    ints_ref[1:2, :] = e1.astype(jnp.int32)
    ints_ref[2:3, :] = rank0.astype(jnp.int32)
    ints_ref[3:4, :] = rank1.astype(jnp.int32)
    carry = carry + c0 + c1
    carry_ref[...] = carry
    cnt_ref[...] = jnp.broadcast_to(carry, cnt_ref.shape)


def _route(scores_t, bias):
    n_e, nt = scores_t.shape
    tr = max(t for t in range(LANES, min(ROUTE_TILE, nt) + 1, LANES) if nt % t == 0)
    tri = jnp.triu(jnp.ones((tr, tr), BF16), k=1)
    return pl.pallas_call(
        functools.partial(_route_kernel, n_groups=N_GROUPS),
        grid=(nt // tr,),
        in_specs=[pl.BlockSpec((n_e, tr), lambda i: (0, i)),
                  pl.BlockSpec((n_e, 1), lambda i: (0, 0)),
                  pl.BlockSpec((tr, tr), lambda i: (0, 0))],
        out_specs=[pl.BlockSpec((4, tr), lambda i: (0, i)),
                   pl.BlockSpec((2, tr), lambda i: (0, i)),
                   pl.BlockSpec((n_e, LANES), lambda i: (0, 0))],
        out_shape=[jax.ShapeDtypeStruct((4, nt), jnp.int32),
                   jax.ShapeDtypeStruct((2, nt), F32),
                   jax.ShapeDtypeStruct((n_e, LANES), F32)],
        scratch_shapes=[pltpu.VMEM((n_e, 1), F32)],
        compiler_params=_cparams(1),
        name="route",
    )(scores_t, bias.reshape(n_e, 1), tri)


def _row_dmas_start(n_rows, make_copy):
    def start(tb, _):
        for u in range(DMA_UNROLL):
            t = tb * DMA_UNROLL + u
            for k in range(2):
                make_copy(t, k).start(priority=(2 * u + k) % 2)
        return 0

    lax.fori_loop(0, n_rows // DMA_UNROLL, start, 0)


def _row_dmas_wait(n_rows, sem_wait_copy):
    def wait(tb, _):
        for _u in range(2 * WAIT_UNROLL):
            sem_wait_copy().wait()
        return 0

    lax.fori_loop(0, n_rows // WAIT_UNROLL, wait, 0)


def _row_dmas(n_rows, make_copy, sem_wait_copy):
    _row_dmas_start(n_rows, make_copy)
    _row_dmas_wait(n_rows, sem_wait_copy)


def _dispatch_kernel(fill_ref, dest_ref, h_ref, xs_ref, zeros_ref, sem, fill_sem, *, td, n_e, tb):
    _row_dmas(td,
              lambda t, k: pltpu.make_async_copy(h_ref.at[pl.ds(t, 1), :],
                                                 xs_ref.at[pl.ds(dest_ref[k * td + t], 1), :], sem),
              lambda: pltpu.make_async_copy(h_ref.at[pl.ds(0, 1), :], xs_ref.at[pl.ds(0, 1), :], sem))

    @pl.when(pl.program_id(0) == 0)
    def _():
        zeros_ref[...] = jnp.zeros_like(zeros_ref)
        bits = [1 << b for b in range(tb.bit_length() - 2, 2, -1)]

        def run(phase, pred, n_rows, off):
            cp = pltpu.make_async_copy(zeros_ref.at[pl.ds(0, n_rows), :], xs_ref.at[pl.ds(off, n_rows), :],
                                       fill_sem)

            @pl.when(pred)
            def _():
                if phase == "start":
                    cp.start()
                else:
                    cp.wait()

        half = tb // 2
        n_half = xs_ref.shape[0] // half

        def tail(phase):
            def body(j, _):
                cp = pltpu.make_async_copy(zeros_ref, xs_ref.at[pl.ds(pl.multiple_of(j * half, half), half), :],
                                           fill_sem)
                if phase == "start":
                    cp.start()
                else:
                    cp.wait()
                return 0

            lax.fori_loop(fill_ref[0, n_e], n_half, body, 0)

        for phase in ("start", "wait"):
            for e in range(n_e):
                first, n_single, off, n = (fill_ref[r, e] for r in range(4))
                for r in range(SUBLANES - 1):
                    run(phase, r < n_single, 1, first + r)
                for bit in bits:
                    run(phase, (n & bit) != 0, bit, pl.multiple_of(off, SUBLANES))
                    off = off + (n & bit)
            tail(phase)


def _dispatch(fill, dest, h2, n_slots, n_e):
    nt, d = h2.shape
    td = min(DMA_TILE, nt)
    tb = MOE_TILE
    gs = pltpu.PrefetchScalarGridSpec(
        num_scalar_prefetch=1,
        grid=(nt // td,),
        in_specs=[pl.BlockSpec((2 * td,), lambda i, f: (i,), memory_space=pltpu.SMEM),
                  pl.BlockSpec((td, d), lambda i, f: (i, 0))],
        out_specs=pl.BlockSpec(memory_space=pl.ANY),
        scratch_shapes=[pltpu.VMEM((tb // 2, d), h2.dtype), pltpu.SemaphoreType.DMA(()),
                        pltpu.SemaphoreType.DMA(())],
    )
    return pl.pallas_call(
        functools.partial(_dispatch_kernel, td=td, n_e=n_e, tb=tb),
        grid_spec=gs,
        out_shape=jax.ShapeDtypeStruct((n_slots, d), h2.dtype),
        compiler_params=_cparams(1),
        name="moe_dispatch",
    )(fill, dest, h2)


def _ffn_kernel(be_ref, bi_ref, nu_ref, xs_ref, wg_ref, wu_ref, wd_ref, y_ref, wgb, wub, wdb):
    j = pl.program_id(0)

    @pl.when(j < nu_ref[0])
    def _():
        prev = be_ref[jnp.maximum(j - 1, 0)]
        changed = jnp.logical_or(j == 0, be_ref[j] != prev)

        @pl.when(changed)
        def _():
            wgb[...] = wg_ref[0, 0].astype(BF16)
            wub[...] = wu_ref[0, 0].astype(BF16)
            wdb[...] = wd_ref[0, 0].astype(BF16)

        x = xs_ref[...].astype(BF16)
        g = jnp.dot(x, wgb[...], preferred_element_type=F32)
        u = jnp.dot(x, wub[...], preferred_element_type=F32)
        h = (g * jax.nn.sigmoid(g)) * u
        y_ref[...] = jnp.dot(h.astype(BF16), wdb[...], preferred_element_type=F32)

    @pl.when(j >= nu_ref[0])
    def _():
        y_ref[...] = jnp.zeros_like(y_ref)


def _ffn(layer, blk_e, blk_in, n_used, xs, w_gate, w_up, w_down):
    n_slots, d = xs.shape
    de = w_gate.shape[-1]
    tb = MOE_TILE
    nb = n_slots // tb
    gs = pltpu.PrefetchScalarGridSpec(
        num_scalar_prefetch=3,
        grid=(nb,),
        in_specs=[pl.BlockSpec((tb, d), lambda j, be, bi, nu: (bi[j], 0)),
                  pl.BlockSpec((1, 1, d, de), lambda j, be, bi, nu: (layer, be[j], 0, 0)),
                  pl.BlockSpec((1, 1, d, de), lambda j, be, bi, nu: (layer, be[j], 0, 0)),
                  pl.BlockSpec((1, 1, de, d), lambda j, be, bi, nu: (layer, be[j], 0, 0))],
        out_specs=pl.BlockSpec((tb, d), lambda j, be, bi, nu: (j, 0)),
        scratch_shapes=[pltpu.VMEM((d, de), BF16), pltpu.VMEM((d, de), BF16), pltpu.VMEM((de, d), BF16)],
    )
    return pl.pallas_call(
        _ffn_kernel,
        grid_spec=gs,
        out_shape=jax.ShapeDtypeStruct((n_slots, d), F32),
        compiler_params=_cparams(1),
        name="moe_ffn",
    )(blk_e, blk_in, n_used, xs, w_gate, w_up, w_down)


def _combine_kernel(dest_ref, dnext_ref, wts_ref, x1_ref, gate_ref, *rest, td, final):
    if final:
        fg_ref, yb_ref, o_ref, ybuf, sems = rest
    else:
        yb_ref, o_ref, ybuf, sems = rest
    i = pl.program_id(0)
    slot = i % 2

    def gather(d_ref, s):
        _row_dmas_start(td, lambda t, k: pltpu.make_async_copy(
            yb_ref.at[pl.ds(d_ref[k * td + t], 1), :], ybuf.at[s, k, pl.ds(t, 1), :], sems.at[s]))

    @pl.when(i == 0)
    def _():
        gather(dest_ref, 0)

    @pl.when(i + 1 < pl.num_programs(0))
    def _():
        gather(dnext_ref, 1 - slot)

    _row_dmas_wait(td, lambda: pltpu.make_async_copy(yb_ref.at[pl.ds(0, 1), :],
                                                     ybuf.at[slot, 0, pl.ds(0, 1), :], sems.at[slot]))
    w = wts_ref[...]
    y = ybuf[slot, 0] * w[:, 0:1] + ybuf[slot, 1] * w[:, 1:2]
    xn = x1_ref[...] + gate_ref[0] * y
    if final:
        ms = jnp.mean(xn * xn, axis=-1, keepdims=True)
        xn = (xn * lax.rsqrt(ms + EPS)) * fg_ref[...]
    o_ref[...] = xn


def _combine(tok, layer, dest, wts_rows, x1, modf, yb, final_g):
    nt, d = x1.shape
    td = tok.tile
    final = final_g is not None
    n_steps = nt // td
    in_specs = [pl.BlockSpec((2 * td,), lambda i: (i,), memory_space=pltpu.SMEM),
                pl.BlockSpec((2 * td,), lambda i: (jnp.minimum(i + 1, n_steps - 1),), memory_space=pltpu.SMEM),
                pl.BlockSpec((td, 2), lambda i: (i, 0)),
                pl.BlockSpec((td, d), lambda i: (i, 0)),
                pl.BlockSpec((1, 1, d), tok.mod_map(layer, 5))]
    args = [dest, dest, wts_rows, x1, modf]
    if final:
        in_specs.append(pl.BlockSpec((1, d), lambda i: (0, 0)))
        args.append(final_g)
    in_specs.append(pl.BlockSpec(memory_space=pl.ANY))
    args.append(yb)
    return pl.pallas_call(
        functools.partial(_combine_kernel, td=td, final=final),
        grid=(nt // td,),
        in_specs=in_specs,
        out_specs=pl.BlockSpec((td, d), lambda i: (i, 0)),
        out_shape=jax.ShapeDtypeStruct((nt, d), F32),
        scratch_shapes=[pltpu.VMEM((2, 2, td, d), F32), pltpu.SemaphoreType.DMA((2,))],
        compiler_params=_cparams(1),
        name="moe_combine",
    )(*args)


def _moe(tok_dma, layer, scores_t, h2, x1, modf, router_bias, w_gate, w_up, w_down, final_g):
    nt = h2.shape[0]
    n_e = w_gate.shape[1]
    tb = MOE_TILE
    ints, wts, cnt = _route(scores_t, router_bias.astype(F32))
    counts = cnt[:, 0].astype(jnp.int32)
    padded = ((counts + tb - 1) // tb) * tb
    eidx = jnp.arange(n_e, dtype=jnp.int32)
    p_end = jnp.sum(jnp.where(eidx[None, :] <= eidx[:, None], padded[None, :], 0), axis=1)
    p_start = (p_end - padded).astype(jnp.int32)
    nb = (2 * nt) // tb + n_e
    n_used = (p_end[-1] // tb).astype(jnp.int32)
    jb = jnp.arange(nb, dtype=jnp.int32)
    be = jnp.sum((p_end[None, :] <= (jb * tb)[:, None]).astype(jnp.int32), axis=1)
    be = jnp.minimum(be, n_e - 1)
    last = jnp.maximum(n_used - 1, 0)
    blk_e = jnp.where(jb < n_used, be, be[last])
    blk_in = jnp.minimum(jb, last)
    seg = jnp.sum(jnp.where(ints[0:2, :, None] == eidx, p_start, 0), axis=-1)
    td = tok_dma.tile
    dest = (seg + ints[2:4]).reshape(2, nt // td, td).transpose(1, 0, 2).reshape(-1)
    pad0 = p_start + counts
    pad8 = jnp.minimum(((pad0 + SUBLANES - 1) // SUBLANES) * SUBLANES, p_end)
    fill = jnp.stack([pad0, pad8 - pad0, pad8, p_end - pad8]).astype(jnp.int32)
    fill = jnp.concatenate([fill, jnp.full((4, 1), 2 * n_used, jnp.int32)], axis=1)
    xs = _dispatch(fill, dest, h2, nb * tb, n_e)
    yb = _ffn(layer, blk_e, blk_in, n_used.reshape(1), xs, w_gate, w_up, w_down)
    return _combine(tok_dma, layer, dest, wts.T, x1, modf, yb, final_g)


def _rope_tables(S, tile):
    n_rows = S // GRID_W
    row = jnp.repeat(jnp.arange(n_rows), GRID_W).astype(F32)
    col = jnp.tile(jnp.arange(GRID_W), n_rows).astype(F32)
    n_freq = 16
    inv = 1.0 / (ROPE_THETA ** (jnp.arange(n_freq, dtype=F32) / n_freq))
    ang = jnp.stack([row, col], axis=-1)[:, :, None] * inv
    ang = jnp.broadcast_to(ang[:, :, None, :], (S, 2, 2, n_freq)).reshape(S, 4 * n_freq)
    cos, sin = jnp.cos(ang), jnp.sin(ang)
    pick = jnp.array(list(range(n_freq)) + list(range(2 * n_freq, 3 * n_freq)))
    cos_t = jnp.concatenate([cos[:, pick].T, jnp.ones((2 * n_freq, tile), F32)], axis=1)
    sin_t = jnp.concatenate([sin[:, pick].T, jnp.zeros((2 * n_freq, tile), F32)], axis=1)
    sign = jnp.where((jnp.arange(4 * n_freq) % 32) < 16, -1.0, 1.0).astype(F32)
    sin = sin * sign
    cos = jnp.concatenate([jnp.tile(cos, (1, 2)), jnp.ones((tile, LANES), F32)], axis=0)
    sin = jnp.concatenate([jnp.tile(sin, (1, 2)), jnp.zeros((tile, LANES), F32)], axis=0)
    return (cos, sin), (cos_t, sin_t)


def kernel(x, c, ctx, c_ctx, w_mod, b_mod, norm1_g, norm2_g, rg_w_in, rg_conv_w, rg_conv_b,
           rg_gate_a_w, rg_gate_a_b, rg_gate_x_w, rg_gate_x_b, rg_lambda, rg_w_out,
           da_w_qkv, da_lambda, da_subln_g, da_w_o, router_w, router_bias,
           moe_w_gate, moe_w_up, moe_w_down, final_g):
    B, S, D = x.shape
    C = ctx.shape[1]
    depth = w_mod.shape[0]
    assert B + 1 <= COND_PAD and D % LANES == 0
    assert all(l % N_MIXERS == 0 or l == depth - 1 for l in range(depth))
    tok = _Tok(B, S, C, min(ROW_TILE, S, B * C))
    tok_dma = _Tok(B, S, C, min(DMA_TILE, S, B * C))

    cond = jnp.zeros((COND_PAD, D), F32).at[:B].set(c).at[B].set(c_ctx)
    mods = _adaln(cond, w_mod, b_mod)
    modf = mods.reshape(depth * COND_PAD * 6, 1, D)
    srcs = (x.reshape(B * S, D), ctx.reshape(B * C, D))
    rw_t = router_w.T.astype(F32)
    rw_hi = rw_t.astype(BF16)
    rw_lo = (rw_t - rw_hi.astype(F32)).astype(BF16)

    i_rg = 0
    i_da = 0
    out = None
    for layer in range(depth):
        last = layer == depth - 1
        n1 = norm1_g[layer].reshape(1, D)
        n2 = norm2_g[layer].reshape(1, D)
        if layer % N_MIXERS == 0:
            d_rnn = rg_w_in.shape[2] // 2
            w_in = rg_w_in[i_rg].astype(BF16)
            y_all, u_all = _rg_in(tok, layer, srcs, n1, modf, w_in[:, :d_rnn], w_in[:, d_rnn:])
            nblk = rg_gate_a_w.shape[2]
            wg = jnp.concatenate([rg_gate_a_w[i_rg, 0], rg_gate_x_w[i_rg, 0],
                                  rg_gate_a_w[i_rg, 1], rg_gate_x_w[i_rg, 1]], axis=-1).astype(BF16)
            bsplit = lambda v: v.reshape(nblk, 1, LANES)
            gb = jnp.concatenate([bsplit(rg_gate_a_b[i_rg, 0]), bsplit(rg_gate_x_b[i_rg, 0]),
                                  bsplit(rg_gate_a_b[i_rg, 1]), bsplit(rg_gate_x_b[i_rg, 1])], axis=-1)
            g_x, g_c = _rglru(B, S, C, u_all, y_all, rg_conv_w[i_rg], rg_conv_b[i_rg].reshape(1, d_rnn),
                              wg, gb, rg_lambda[i_rg])
            w_o = rg_w_out[i_rg].astype(BF16)
            i_rg += 1
        else:
            dh = da_lambda.shape[2]
            heads = D // (2 * dh)
            lambda_init = 0.8 - 0.6 * math.exp(-0.3 * layer)
            wqkv = da_w_qkv[i_da].astype(BF16)
            tables, tables_t = _rope_tables(S, tok.tile)
            tc = min(ATTN_KEY_TILE, S, C)
            tq = min(ATTN_Q_TILE, S, tok.tile)
            assert S % tc == 0 and C % tc == 0 and tok.tile % tc == 0 and S % tq == 0 and tok.tile % tq == 0
            xall = srcs[0] if len(srcs) == 1 else jnp.concatenate(srcs, axis=0)
            qt_all, k_all, vt_all = _qkv(tok, layer, xall, n1, modf, wqkv[:, :D].T, wqkv[:, D:2 * D],
                                         wqkv[:, 2 * D:].T, tables, tables_t, tq, tc,
                                         math.log2(math.e) * dh ** -0.5)
            g_x = _attention(B, S, C, heads, qt_all, k_all, vt_all, da_lambda[i_da],
                             da_subln_g[i_da].reshape(2 * dh, 1), lambda_init)
            g_c = None
            w_o = da_w_o[i_da].astype(BF16)
            i_da += 1
        if last:
            n_rows = B * S
            g_srcs = (g_x,)
        else:
            n_rows = B * S + B * C
            g_srcs = (g_x, g_c)
        x1, h2, scores_t = _proj_router(tok, layer, n_rows, g_srcs, w_o, srcs, n2, modf, rw_hi, rw_lo)
        res = _moe(tok_dma, layer, scores_t, h2, x1, modf, router_bias,
                   moe_w_gate, moe_w_up, moe_w_down,
                   final_g.reshape(1, D) if last else None)
        if last:
            out = res.reshape(B, S, D)
        else:
            srcs = (res,)
    return out
```

```python
import functools
import math

import jax
import jax.numpy as jnp
from jax import lax
from jax.experimental import pallas as pl
from jax.experimental.pallas import tpu as pltpu

F32 = jnp.float32
BF16 = jnp.bfloat16

EPS = 1e-6
TINY = 1e-30
GRID_W = 64
CONV_W = 4
CONV_LEFT = 2
RG_C = 8.0
ROPE_THETA = 10000.0
N_GROUPS = 4
N_MIXERS = 2

LANES = 128
SUBLANES = 8
VMEM_LIMIT = 56 * 1024 * 1024

ROW_TILE = 512
MOE_TILE = 256
DMA_TILE = 512
DMA_UNROLL = 8
WAIT_UNROLL = 32
ROUTE_TILE = 1024
ATTN_Q_TILE = 256
ATTN_KEY_TILE = 256
ATTN_KEY_GROUP = 1
ATTN_HEADS_PER_STEP = 2
ATTN_SHIFT_SLACK = 64.0
SCAN_CHUNK = 256
SCAN_UNROLL = 8
COND_PAD = 8


def _cparams(n_axes):
    return pltpu.CompilerParams(dimension_semantics=("arbitrary",) * n_axes,
                                vmem_limit_bytes=VMEM_LIMIT)


def _adaln_kernel(cond_ref, w_ref, b_ref, o_ref):
    c = cond_ref[...]
    h = c * jax.nn.sigmoid(c)
    o_ref[0] = jnp.dot(h, w_ref[0], preferred_element_type=F32,
                       precision=lax.Precision.HIGHEST) + b_ref[0]


def _adaln(cond, w_mod, b_mod):
    depth, d, n6 = w_mod.shape
    tn = n6 // 6
    return pl.pallas_call(
        _adaln_kernel,
        grid=(depth, n6 // tn),
        in_specs=[pl.BlockSpec((COND_PAD, d), lambda l, j: (0, 0)),
                  pl.BlockSpec((1, d, tn), lambda l, j: (l, 0, j)),
                  pl.BlockSpec((1, 1, tn), lambda l, j: (l, 0, j))],
        out_specs=pl.BlockSpec((1, COND_PAD, tn), lambda l, j: (l, 0, j)),
        out_shape=jax.ShapeDtypeStruct((depth, COND_PAD, n6), F32),
        compiler_params=_cparams(2),
        name="adaln",
    )(cond, w_mod, b_mod.reshape(depth, 1, n6))


def _norm_mod(x, g, shift, scale):
    ms = jnp.mean(x * x, axis=-1, keepdims=True)
    h = (x * lax.rsqrt(ms + EPS)) * g
    return h * (1.0 + scale) + shift


class _Tok:
    def __init__(self, B, S, C, tile):
        assert S % tile == 0 and (B * C) % tile == 0
        self.B, self.S, self.C, self.tile = B, S, C, tile
        self.nx = B * S // tile
        self.nc = B * C // tile
        self.spt = S // tile

    def cond(self, i):
        return jnp.where(i < self.nx, i // self.spt, self.B)

    def mod_map(self, layer, chunk):
        return lambda i, *_: ((layer * COND_PAD + self.cond(i)) * 6 + chunk, 0, 0)


def _token_specs(tok, srcs):
    tm, d = tok.tile, srcs[0].shape[1]
    if len(srcs) == 1:
        return [pl.BlockSpec((tm, d), lambda i, *_: (i, 0))]
    return [pl.BlockSpec((tm, d), lambda i, *_: (jnp.minimum(i, tok.nx - 1), 0)),
            pl.BlockSpec((tm, d), lambda i, *_: (jnp.maximum(i - tok.nx, 0), 0))]


def _token_tile(x_refs, nx):
    if len(x_refs) == 1:
        return x_refs[0][...]
    return jnp.where(pl.program_id(0) < nx, x_refs[0][...], x_refs[1][...])


def _dot_nt(a, b):
    return lax.dot_general(a, b, (((1,), (1,)), ((), ())), preferred_element_type=F32)


def _rg_in_kernel(*refs, n_src, nx):
    x_refs, (g_ref, sh_ref, sc_ref, wy_ref, wu_ref, y_ref, u_ref) = refs[:n_src], refs[n_src:]
    h = _norm_mod(_token_tile(x_refs, nx), g_ref[...], sh_ref[0], sc_ref[0]).astype(BF16)
    y = jnp.dot(h, wy_ref[...], preferred_element_type=F32)
    y_ref[...] = jax.nn.gelu(y, approximate=True).astype(y_ref.dtype)
    u_ref[...] = jnp.dot(h, wu_ref[...], preferred_element_type=F32)


def _rg_in(tok, layer, srcs, g, modf, w_y, w_u):
    d = srcs[0].shape[1]
    tm = tok.tile
    ntok = (tok.nx + tok.nc) * tm
    nw = w_y.shape[1]
    return pl.pallas_call(
        functools.partial(_rg_in_kernel, n_src=len(srcs), nx=tok.nx),
        grid=(ntok // tm,),
        in_specs=_token_specs(tok, srcs) + [
            pl.BlockSpec((1, d), lambda i: (0, 0)),
            pl.BlockSpec((1, 1, d), tok.mod_map(layer, 0)),
            pl.BlockSpec((1, 1, d), tok.mod_map(layer, 1)),
            pl.BlockSpec((d, nw), lambda i: (0, 0)),
            pl.BlockSpec((d, nw), lambda i: (0, 0))],
        out_specs=[pl.BlockSpec((tm, nw), lambda i: (i, 0)), pl.BlockSpec((tm, nw), lambda i: (i, 0))],
        out_shape=[jax.ShapeDtypeStruct((ntok, nw), BF16), jax.ShapeDtypeStruct((ntok, nw), F32)],
        compiler_params=_cparams(1),
        name="rg_in_proj",
    )(*srcs, g, modf, modf, w_y, w_u)


def _qkv_kernel(x_ref, g_ref, sh_ref, sc_ref, wqt_ref, wk_ref, wvt_ref, cos_ref, sin_ref, cost_ref, sint_ref,
                qt_ref, k_ref, vt_ref, *, q_scale):
    h = _norm_mod(x_ref[...], g_ref[...], sh_ref[0], sc_ref[0]).astype(BF16)

    qt = _dot_nt(wqt_ref[...], h)
    cos = cost_ref[...]
    sin = sint_ref[...]
    n_q, nw, tq = qt_ref.shape
    for g in range(nw // 32):
        a = qt[g * 32:g * 32 + 16]
        b = qt[g * 32 + 16:g * 32 + 32]
        c = cos[(g % 2) * 16:(g % 2) * 16 + 16]
        s = sin[(g % 2) * 16:(g % 2) * 16 + 16]
        ra = ((a * c - b * s) * q_scale).astype(qt_ref.dtype)
        rb = ((b * c + a * s) * q_scale).astype(qt_ref.dtype)
        for ci in range(n_q):
            qt_ref[ci, g * 32:g * 32 + 16, :] = ra[:, ci * tq:(ci + 1) * tq]
            qt_ref[ci, g * 32 + 16:g * 32 + 32, :] = rb[:, ci * tq:(ci + 1) * tq]

    k = jnp.dot(h, wk_ref[...], preferred_element_type=F32)
    cos = cos_ref[...]
    sin = sin_ref[...]
    lane = lax.broadcasted_iota(jnp.int32, cos.shape, 1)
    first = (lane % 32) < 16
    for cb in range(k.shape[1] // LANES):
        a = k[:, cb * LANES:(cb + 1) * LANES]
        rot = jnp.where(first, pltpu.roll(a, LANES - 16, 1), pltpu.roll(a, 16, 1))
        k_ref[:, cb * LANES:(cb + 1) * LANES] = (a * cos + rot * sin).astype(k_ref.dtype)

    vt = _dot_nt(wvt_ref[...], h)
    n_v, _, tc = vt_ref.shape
    for ci in range(n_v):
        vt_ref[ci] = vt[:, ci * tc:(ci + 1) * tc].astype(vt_ref.dtype)


def _qkv(tok, layer, xall, g, modf, wqt, wk, wvt, tables, tables_t, tq, tc, q_scale):
    ntok, d = xall.shape
    tm = tok.tile
    tmap = lambda i: (jnp.where(i < tok.nx, i % tok.spt, tok.spt), 0)
    tmap_t = lambda i: (0, jnp.where(i < tok.nx, i % tok.spt, tok.spt))
    wspec = pl.BlockSpec((d, d), lambda i: (0, 0))
    return pl.pallas_call(
        functools.partial(_qkv_kernel, q_scale=q_scale),
        grid=(ntok // tm,),
        in_specs=[pl.BlockSpec((tm, d), lambda i: (i, 0)),
                  pl.BlockSpec((1, d), lambda i: (0, 0)),
                  pl.BlockSpec((1, 1, d), tok.mod_map(layer, 0)),
                  pl.BlockSpec((1, 1, d), tok.mod_map(layer, 1)),
                  wspec, wspec, wspec,
                  pl.BlockSpec((tm, LANES), tmap), pl.BlockSpec((tm, LANES), tmap),
                  pl.BlockSpec((32, tm), tmap_t), pl.BlockSpec((32, tm), tmap_t)],
        out_specs=[pl.BlockSpec((tm // tq, d, tq), lambda i: (i, 0, 0)),
                   pl.BlockSpec((tm, d), lambda i: (i, 0)),
                   pl.BlockSpec((tm // tc, d, tc), lambda i: (i, 0, 0))],
        out_shape=[jax.ShapeDtypeStruct((ntok // tq, d, tq), BF16),
                   jax.ShapeDtypeStruct((ntok, d), BF16),
                   jax.ShapeDtypeStruct((ntok // tc, d, tc), BF16)],
        compiler_params=_cparams(1),
        name="qkv_proj",
    )(xall, g, modf, modf, wqt, wk, wvt, *tables, *tables_t)


def _softplus(x):
    return jnp.maximum(x, 0.0) + jnp.log(1.0 + jnp.exp(-jnp.abs(x)))


def _rglru_kernel(ux_ref, uc_ref, yx_ref, yc_ref, cw_ref, cb_ref, wg_ref, gb_ref, lam_ref,
                  ox_ref, oc_ref, a_s, b_s, hf_s, hb_s, *, S, C, chunk):
    ka = _softplus(-lam_ref[...]) * (-RG_C * math.log2(math.e))
    cw = cw_ref[...]
    cb = cb_ref[...]
    wg = wg_ref[0]
    gb = gb_ref[0]

    def coeffs(u_ref, T):
        rc = min(chunk, T)

        def body(ci, _):
            c0 = pl.multiple_of(ci * rc, rc)
            cur = u_ref[pl.ds(c0, rc), :]
            pstart = pl.multiple_of(jnp.maximum(c0 - SUBLANES, 0), SUBLANES)
            nstart = pl.multiple_of(jnp.minimum(c0 + rc, T - SUBLANES), SUBLANES)
            prev = jnp.where(c0 > 0, u_ref[pl.ds(pstart, SUBLANES), :], 0.0)
            nxt = jnp.where(c0 + rc < T, u_ref[pl.ds(nstart, SUBLANES), :], 0.0)
            ext = jnp.concatenate([prev, cur, nxt], axis=0)
            n_ext = rc + 2 * SUBLANES
            xm2 = pltpu.roll(ext, 2, 0)[SUBLANES:SUBLANES + rc]
            xm1 = pltpu.roll(ext, 1, 0)[SUBLANES:SUBLANES + rc]
            xp1 = pltpu.roll(ext, n_ext - 1, 0)[SUBLANES:SUBLANES + rc]
            u = xm2 * cw[0:1] + xm1 * cw[1:2] + cur * cw[2:3] + xp1 * cw[3:4] + cb
            z = jnp.dot(u.astype(BF16), wg, preferred_element_type=F32) + gb
            for d in range(2):
                r = jax.nn.sigmoid(z[:, d * 2 * LANES:d * 2 * LANES + LANES])
                ig = jax.nn.sigmoid(z[:, d * 2 * LANES + LANES:(d + 1) * 2 * LANES])
                a = jnp.exp2(r * ka[d:d + 1])
                om = 1.0 - a * a
                root = om * lax.rsqrt(jnp.maximum(om, TINY))
                bcoef = root * (ig * u)
                a_s[d, pl.ds(c0, rc), :] = a
                b_s[d, pl.ds(c0, rc), :] = bcoef
            return 0

        lax.fori_loop(0, T // rc, body, 0)

    row = lax.broadcasted_iota(jnp.int32, (SUBLANES, LANES), 0)

    def scans(T, cf, cbk):
        def body(k, carry):
            cf, cbk = carry
            rf = pl.multiple_of(k * SUBLANES, SUBLANES)
            rb = pl.multiple_of(T - SUBLANES - k * SUBLANES, SUBLANES)
            a = a_s[0, pl.ds(rf, SUBLANES), :]
            b = b_s[0, pl.ds(rf, SUBLANES), :]
            for s in (1, 2, 4):
                m = row >= s
                b = jnp.where(m, a * pltpu.roll(b, s, 0), 0.0) + b
                a = jnp.where(m, a * pltpu.roll(a, s, 0), a)
            h = b + a * cf
            hf_s[pl.ds(rf, SUBLANES), :] = h
            cf = h[SUBLANES - 1:SUBLANES, :]
            a = a_s[1, pl.ds(rb, SUBLANES), :]
            b = b_s[1, pl.ds(rb, SUBLANES), :]
            for s in (1, 2, 4):
                m = row < SUBLANES - s
                b = jnp.where(m, a * pltpu.roll(b, SUBLANES - s, 0), 0.0) + b
                a = jnp.where(m, a * pltpu.roll(a, SUBLANES - s, 0), a)
            h = b + a * cbk
            hb_s[pl.ds(rb, SUBLANES), :] = h
            cbk = h[0:1, :]
            return cf, cbk

        return lax.fori_loop(0, T // SUBLANES, body, (cf, cbk), unroll=SCAN_UNROLL)

    zero = jnp.zeros((1, LANES), F32)
    coeffs(uc_ref, C)
    scans(C, zero, zero)
    oc_ref[...] = ((hf_s[0:C, :] + hb_s[0:C, :]) * yc_ref[...].astype(F32)).astype(oc_ref.dtype)
    h0f = hf_s[C - 1:C, :]
    h0b = hb_s[0:1, :]
    coeffs(ux_ref, S)
    scans(S, h0f, h0b)
    ox_ref[...] = ((hf_s[0:S, :] + hb_s[0:S, :]) * yx_ref[...].astype(F32)).astype(ox_ref.dtype)


def _rglru(B, S, C, u_all, y_all, conv_w, conv_b, wg, gb, lam):
    d_rnn = u_all.shape[1]
    nblk = d_rnn // LANES
    cbase = B * S // C
    tmax = max(S, C)
    return pl.pallas_call(
        functools.partial(_rglru_kernel, S=S, C=C, chunk=SCAN_CHUNK),
        grid=(B, nblk),
        in_specs=[pl.BlockSpec((S, LANES), lambda b, n: (b, n)),
                  pl.BlockSpec((C, LANES), lambda b, n: (cbase + b, n)),
                  pl.BlockSpec((S, LANES), lambda b, n: (b, n)),
                  pl.BlockSpec((C, LANES), lambda b, n: (cbase + b, n)),
                  pl.BlockSpec((CONV_W, LANES), lambda b, n: (0, n)),
                  pl.BlockSpec((1, LANES), lambda b, n: (0, n)),
                  pl.BlockSpec((1, LANES, 4 * LANES), lambda b, n: (n, 0, 0)),
                  pl.BlockSpec((1, 1, 4 * LANES), lambda b, n: (n, 0, 0)),
                  pl.BlockSpec((2, LANES), lambda b, n: (0, n))],
        out_specs=[pl.BlockSpec((S, LANES), lambda b, n: (b, n)),
                   pl.BlockSpec((C, LANES), lambda b, n: (b, n))],
        out_shape=[jax.ShapeDtypeStruct((B * S, d_rnn), BF16),
                   jax.ShapeDtypeStruct((B * C, d_rnn), BF16)],
        scratch_shapes=[pltpu.VMEM((2, tmax, LANES), F32), pltpu.VMEM((2, tmax, LANES), F32),
                        pltpu.VMEM((tmax, LANES), F32), pltpu.VMEM((tmax, LANES), F32)],
        compiler_params=_cparams(2),
        name="rglru",
    )(u_all, u_all, y_all, y_all, conv_w, conv_b, wg, gb, lam)


def _attn_kernel(q_ref, kx_ref, kc_ref, vx_ref, vc_ref, lam_ref, g_ref, o_ref, *s_refs, lambda_init):
    dh2 = g_ref.shape[0]
    n_h = q_ref.shape[1] // dh2
    tq = q_ref.shape[2]
    tk = kx_ref.shape[0] // vx_ref.shape[0]
    n_x = vx_ref.shape[0]
    group = ATTN_KEY_GROUP
    assert n_x % group == 0 and 0 < vc_ref.shape[0] <= group
    n_g = n_x // group
    gk = group * tk
    n_c = vc_ref.shape[0] * tk
    heads = range(n_h)
    hs = [slice(hh * dh2, (hh + 1) * dh2) for hh in heads]

    qz = []
    for hh in heads:
        qt = q_ref[0, hs[hh], :]
        row = lax.broadcasted_iota(jnp.int32, qt.shape, 0)
        zero = jnp.zeros_like(qt)
        qz.append(jnp.concatenate([jnp.where(row < dh2 // 2, qt, zero), jnp.where(row >= dh2 // 2, qt, zero)],
                                  axis=1))

    def scores(hh, kblk):
        return jnp.dot(kblk, qz[hh], preferred_element_type=F32)

    def xkeys(hh, g):
        return kx_ref[g * gk:(g + 1) * gk, hs[hh]]

    def xvals(hh, g):
        return jnp.concatenate([vx_ref[g * group + c, hs[hh], :] for c in range(group)], axis=1)

    def cvals(hh):
        return jnp.concatenate([vc_ref[c, hs[hh], :] for c in range(vc_ref.shape[0])], axis=1)

    lv = lam_ref[...]
    lam = (jnp.exp(jnp.sum(lv[0:1] * lv[1:2], axis=-1, keepdims=True))
           - jnp.exp(jnp.sum(lv[2:3] * lv[3:4], axis=-1, keepdims=True)) + lambda_init)

    def finish(hh, l, acc):
        ot = acc[:, :tq] / l[:, :tq] - lam * (acc[:, tq:] / l[:, tq:])
        ms = jnp.mean(ot * ot, axis=0, keepdims=True)
        ot = (ot * lax.rsqrt(ms + EPS)) * g_ref[...] * (1.0 - lambda_init)
        o_ref[:, hs[hh]] = ot.T.astype(o_ref.dtype)

    bufs = [(s_refs[2 * hh], s_refs[2 * hh + 1]) for hh in heads]
    m, l, acc, top = {}, {}, {}, {}
    sc = [scores(hh, kc_ref[:, hs[hh]]) for hh in heads]
    for hh in heads:
        bufs[hh][0][...] = scores(hh, xkeys(hh, 0))
    for hh in heads:
        m[hh] = jnp.max(sc[hh], axis=0, keepdims=True)
        p = jnp.exp2(sc[hh] - m[hh])
        l[hh] = jnp.sum(p, axis=0, keepdims=True)
        acc[hh] = jnp.dot(cvals(hh), p.astype(BF16), preferred_element_type=F32)
        top[hh] = m[hh]
    for g in range(n_g):
        for hh in heads:
            if g + 1 < n_g:
                bufs[hh][(g + 1) % 2][...] = scores(hh, xkeys(hh, g + 1))
            s = bufs[hh][g % 2][...]
            top[hh] = jnp.maximum(top[hh], jnp.max(s, axis=0, keepdims=True))
            p = jnp.exp2(s - m[hh])
            l[hh] = l[hh] + jnp.sum(p, axis=0, keepdims=True)
            acc[hh] = acc[hh] + jnp.dot(xvals(hh, g), p.astype(BF16), preferred_element_type=F32)
    for hh in heads:
        finish(hh, l[hh], acc[hh])

    def update(st, s, vt):
        mo, lo, ao = st
        mn = jnp.maximum(mo, jnp.max(s, axis=0, keepdims=True))
        alpha = jnp.exp2(mo - mn)
        p = jnp.exp2(s - mn)
        lo = alpha * lo + jnp.sum(p, axis=0, keepdims=True)
        ao = alpha * ao + jnp.dot(vt, p.astype(BF16), preferred_element_type=F32)
        return mn, lo, ao

    for hh in heads:
        excess = jnp.max(top[hh] - m[hh])

        @pl.when(jnp.logical_not(excess <= ATTN_SHIFT_SLACK))
        def _(hh=hh):
            st = (jnp.full((1, 2 * tq), -jnp.inf, F32), jnp.zeros((1, 2 * tq), F32),
                  jnp.zeros((dh2, 2 * tq), F32))
            bh = bufs[hh]
            bh[0][...] = scores(hh, xkeys(hh, 0))
            for g in range(n_g):
                nxt = bh[(g + 1) % 2]
                if g + 1 < n_g:
                    nxt[...] = scores(hh, xkeys(hh, g + 1))
                else:
                    nxt[0:n_c, :] = scores(hh, kc_ref[:, hs[hh]])
                st = update(st, bh[g % 2][...], xvals(hh, g))
            st = update(st, bh[n_g % 2][0:n_c, :], cvals(hh))
            finish(hh, st[1], st[2])


def _attention(B, S, C, heads, qt_all, k_all, vt_all, lam, subln_g, lambda_init):
    d = k_all.shape[1]
    dh2 = d // heads
    tq = qt_all.shape[2]
    tc = vt_all.shape[2]
    qpb = S // tq
    cbase = B * S // C
    hp = ATTN_HEADS_PER_STEP if heads % ATTN_HEADS_PER_STEP == 0 else 1
    hw = hp * dh2
    return pl.pallas_call(
        functools.partial(_attn_kernel, lambda_init=lambda_init),
        grid=(B, heads // hp, qpb),
        in_specs=[pl.BlockSpec((1, hw, tq), lambda b, h, i: (b * qpb + i, h, 0)),
                  pl.BlockSpec((S, hw), lambda b, h, i: (b, h)),
                  pl.BlockSpec((C, hw), lambda b, h, i: (cbase + b, h)),
                  pl.BlockSpec((S // tc, hw, tc), lambda b, h, i: (b, h, 0)),
                  pl.BlockSpec((C // tc, hw, tc), lambda b, h, i: (cbase + b, h, 0)),
                  pl.BlockSpec(lam.shape, lambda b, h, i: (0, 0)),
                  pl.BlockSpec((dh2, 1), lambda b, h, i: (0, 0))],
        out_specs=pl.BlockSpec((tq, hw), lambda b, h, i: (b * qpb + i, h)),
        out_shape=jax.ShapeDtypeStruct((B * S, d), BF16),
        scratch_shapes=[pltpu.VMEM((ATTN_KEY_GROUP * tc, 2 * tq), F32) for _ in range(2 * hp)],
        compiler_params=_cparams(3),
        name="diff_attn",
    )(qt_all, k_all, k_all, vt_all, vt_all, lam, subln_g)


def _proj_router_kernel(*refs, n_g, n_x, nx):
    g_refs, x_refs, refs = refs[:n_g], refs[n_g:n_g + n_x], refs[n_g + n_x:]
    (w_ref, gate_ref, g2_ref, sh_ref, sc_ref, rwh_ref, rwl_ref, x1_ref, h2_ref, st_ref) = refs
    o = jnp.dot(_token_tile(g_refs, nx), w_ref[...], preferred_element_type=F32)
    x1 = _token_tile(x_refs, nx) + gate_ref[0] * o
    x1_ref[...] = x1
    h2 = _norm_mod(x1, g2_ref[...], sh_ref[0], sc_ref[0])
    h2_ref[...] = _pack_pairs(h2)
    hi = h2.astype(BF16)
    lo = (h2 - hi.astype(F32)).astype(BF16)
    rwh = rwh_ref[...]
    logits = _dot_nt(rwh, hi) + _dot_nt(rwh, lo) + _dot_nt(rwl_ref[...], hi)
    st_ref[...] = jax.nn.sigmoid(logits)


def _proj_router(tok, layer, n_rows, g_srcs, w, x_srcs, norm2_g, modf, rw_hi, rw_lo):
    d = w.shape[1]
    tm = tok.tile
    e = rw_hi.shape[0]
    in_specs = _token_specs(tok, g_srcs) + _token_specs(tok, x_srcs)
    in_specs += [pl.BlockSpec((d, d), lambda i: (0, 0)),
                 pl.BlockSpec((1, 1, d), tok.mod_map(layer, 2)),
                 pl.BlockSpec((1, d), lambda i: (0, 0)),
                 pl.BlockSpec((1, 1, d), tok.mod_map(layer, 3)),
                 pl.BlockSpec((1, 1, d), tok.mod_map(layer, 4)),
                 pl.BlockSpec((e, d), lambda i: (0, 0)),
                 pl.BlockSpec((e, d), lambda i: (0, 0))]
    args = [*g_srcs, *x_srcs, w, modf, norm2_g, modf, modf, rw_hi, rw_lo]
    return pl.pallas_call(
        functools.partial(_proj_router_kernel, n_g=len(g_srcs), n_x=len(x_srcs), nx=tok.nx),
        grid=(n_rows // tm,),
        in_specs=in_specs,
        out_specs=[pl.BlockSpec((tm, d), lambda i: (i, 0)),
                   pl.BlockSpec((tm, d // 2), lambda i: (i, 0)),
                   pl.BlockSpec((e, tm), lambda i: (0, i))],
        out_shape=[jax.ShapeDtypeStruct((n_rows, d), F32),
                   jax.ShapeDtypeStruct((n_rows, d // 2), jnp.uint32),
                   jax.ShapeDtypeStruct((e, n_rows), F32)],
        compiler_params=_cparams(1),
        name="proj_router",
    )(*args)


def _route_kernel(s_ref, bias_ref, u_ref, ints_ref, wts_ref, cnt_ref, carry_ref, *, n_groups):
    i = pl.program_id(0)

    @pl.when(i == 0)
    def _():
        carry_ref[...] = jnp.zeros_like(carry_ref)

    s = s_ref[...]
    n_e, tr = s.shape
    epg = n_e // n_groups
    sel = s + bias_ref[...]
    ridx = lax.broadcasted_iota(jnp.int32, (epg, tr), 0).astype(F32)
    big = float(epg)
    best = gi = bi1 = bi2 = None
    for g in range(n_groups):
        sg = sel[g * epg:(g + 1) * epg, :]
        m1 = jnp.max(sg, axis=0, keepdims=True)
        i1 = jnp.min(jnp.where(sg == m1, ridx, big), axis=0, keepdims=True)
        sg2 = jnp.where(ridx == i1, -jnp.inf, sg)
        m2 = jnp.max(sg2, axis=0, keepdims=True)
        i2 = jnp.min(jnp.where(sg2 == m2, ridx, big), axis=0, keepdims=True)
        score = m1 + m2
        if g == 0:
            best, gi, bi1, bi2 = score, jnp.zeros_like(i1), i1, i2
        else:
            better = score > best
            best = jnp.where(better, score, best)
            gi = jnp.where(better, float(g), gi)
            bi1 = jnp.where(better, i1, bi1)
            bi2 = jnp.where(better, i2, bi2)
    e0 = gi * epg + bi1
    e1 = gi * epg + bi2
    eidx = lax.broadcasted_iota(jnp.int32, (n_e, tr), 0).astype(F32)
    oh0 = eidx == e0
    oh1 = eidx == e1
    w0 = jnp.sum(jnp.where(oh0, s, 0.0), axis=0, keepdims=True)
    w1 = jnp.sum(jnp.where(oh1, s, 0.0), axis=0, keepdims=True)
    den = w0 + w1
    wts_ref[0:1, :] = w0 / den
    wts_ref[1:2, :] = w1 / den
    f0 = jnp.where(oh0, 1.0, 0.0)
    f1 = jnp.where(oh1, 1.0, 0.0)
    u = u_ref[...]
    p0 = jnp.dot(f0.astype(BF16), u, preferred_element_type=F32)
    p1 = jnp.dot(f1.astype(BF16), u, preferred_element_type=F32)
    carry = carry_ref[...]
    c0 = jnp.sum(f0, axis=1, keepdims=True)
    c1 = jnp.sum(f1, axis=1, keepdims=True)
    rank0 = jnp.sum(f0 * (carry + p0), axis=0, keepdims=True)
    rank1 = jnp.sum(f1 * (carry + c0 + p1), axis=0, keepdims=True)
    ints_ref[0:1, :] = e0.astype(jnp.int32)
    ints_ref[1:2, :] = e1.astype(jnp.int32)
    ints_ref[2:3, :] = rank0.astype(jnp.int32)
    ints_ref[3:4, :] = rank1.astype(jnp.int32)
    carry = carry + c0 + c1
    carry_ref[...] = carry
    cnt_ref[...] = jnp.broadcast_to(carry, cnt_ref.shape)


def _route(scores_t, bias):
    n_e, nt = scores_t.shape
    tr = max(t for t in range(LANES, min(ROUTE_TILE, nt) + 1, LANES) if nt % t == 0)
    tri = jnp.triu(jnp.ones((tr, tr), BF16), k=1)
    return pl.pallas_call(
        functools.partial(_route_kernel, n_groups=N_GROUPS),
        grid=(nt // tr,),
        in_specs=[pl.BlockSpec((n_e, tr), lambda i: (0, i)),
                  pl.BlockSpec((n_e, 1), lambda i: (0, 0)),
                  pl.BlockSpec((tr, tr), lambda i: (0, 0))],
        out_specs=[pl.BlockSpec((4, tr), lambda i: (0, i)),
                   pl.BlockSpec((2, tr), lambda i: (0, i)),
                   pl.BlockSpec((n_e, LANES), lambda i: (0, 0))],
        out_shape=[jax.ShapeDtypeStruct((4, nt), jnp.int32),
                   jax.ShapeDtypeStruct((2, nt), F32),
                   jax.ShapeDtypeStruct((n_e, LANES), F32)],
        scratch_shapes=[pltpu.VMEM((n_e, 1), F32)],
        compiler_params=_cparams(1),
        name="route",
    )(scores_t, bias.reshape(n_e, 1), tri)


def _row_dmas_start(n_rows, make_copy):
    def start(tb, _):
        for u in range(DMA_UNROLL):
            t = tb * DMA_UNROLL + u
            for k in range(2):
                make_copy(t, k).start(priority=(2 * u + k) % 2)
        return 0

    lax.fori_loop(0, n_rows // DMA_UNROLL, start, 0)


def _row_dmas_wait(n_rows, sem_wait_copy):
    def wait(tb, _):
        for _u in range(2 * WAIT_UNROLL):
            sem_wait_copy().wait()
        return 0

    lax.fori_loop(0, n_rows // WAIT_UNROLL, wait, 0)


def _row_dmas(n_rows, make_copy, sem_wait_copy):
    _row_dmas_start(n_rows, make_copy)
    _row_dmas_wait(n_rows, sem_wait_copy)


def _dispatch_kernel(fill_ref, dest_ref, h_ref, xs_ref, zeros_ref, sem, fill_sem, *, td, n_e, tb):
    _row_dmas(td,
              lambda t, k: pltpu.make_async_copy(h_ref.at[pl.ds(t, 1), :],
                                                 xs_ref.at[pl.ds(dest_ref[k * td + t], 1), :], sem),
              lambda: pltpu.make_async_copy(h_ref.at[pl.ds(0, 1), :], xs_ref.at[pl.ds(0, 1), :], sem))

    @pl.when(pl.program_id(0) == 0)
    def _():
        zeros_ref[...] = jnp.zeros_like(zeros_ref)
        bits = [1 << b for b in range(tb.bit_length() - 2, 2, -1)]

        def run(phase, pred, n_rows, off):
            cp = pltpu.make_async_copy(zeros_ref.at[pl.ds(0, n_rows), :], xs_ref.at[pl.ds(off, n_rows), :],
                                       fill_sem)

            @pl.when(pred)
            def _():
                if phase == "start":
                    cp.start()
                else:
                    cp.wait()

        half = tb // 2
        n_half = xs_ref.shape[0] // half

        def tail(phase):
            def body(j, _):
                cp = pltpu.make_async_copy(zeros_ref, xs_ref.at[pl.ds(pl.multiple_of(j * half, half), half), :],
                                           fill_sem)
                if phase == "start":
                    cp.start()
                else:
                    cp.wait()
                return 0

            lax.fori_loop(fill_ref[0, n_e], n_half, body, 0)

        for phase in ("start", "wait"):
            for e in range(n_e):
                first, n_single, off, n = (fill_ref[r, e] for r in range(4))
                for r in range(SUBLANES - 1):
                    run(phase, r < n_single, 1, first + r)
                for bit in bits:
                    run(phase, (n & bit) != 0, bit, pl.multiple_of(off, SUBLANES))
                    off = off + (n & bit)
            tail(phase)


def _dispatch(fill, dest, h2, n_slots, n_e):
    nt, d = h2.shape
    td = min(DMA_TILE, nt)
    tb = MOE_TILE
    gs = pltpu.PrefetchScalarGridSpec(
        num_scalar_prefetch=1,
        grid=(nt // td,),
        in_specs=[pl.BlockSpec((2 * td,), lambda i, f: (i,), memory_space=pltpu.SMEM),
                  pl.BlockSpec((td, d), lambda i, f: (i, 0))],
        out_specs=pl.BlockSpec(memory_space=pl.ANY),
        scratch_shapes=[pltpu.VMEM((tb // 2, d), h2.dtype), pltpu.SemaphoreType.DMA(()),
                        pltpu.SemaphoreType.DMA(())],
    )
    return pl.pallas_call(
        functools.partial(_dispatch_kernel, td=td, n_e=n_e, tb=tb),
        grid_spec=gs,
        out_shape=jax.ShapeDtypeStruct((n_slots, d), h2.dtype),
        compiler_params=_cparams(1),
        name="moe_dispatch",
    )(fill, dest, h2)


def _pack_pairs(x):
    n = x.shape[1] // 2
    hi = lax.bitcast_convert_type(x[:, :n].astype(BF16).astype(F32), jnp.uint32)
    lo = lax.bitcast_convert_type(x[:, n:].astype(BF16).astype(F32), jnp.uint32)
    return hi | (lo >> 16)


def _unpack_pairs(w):
    a = lax.bitcast_convert_type(w & jnp.uint32(0xFFFF0000), F32)
    b = lax.bitcast_convert_type(w << 16, F32)
    return a, b


def _ffn_kernel(be_ref, bi_ref, nu_ref, seg_ref, nxt_ref, xs_ref, wg_hbm, wu_hbm, wd_hbm, y_ref,
                wgs, wus, wds, wgb, wub, wdb, sems, *, layer):
    j = pl.program_id(0)

    def fetch(e, slot):
        return (pltpu.make_async_copy(wg_hbm.at[layer, e], wgs.at[slot], sems.at[slot]),
                pltpu.make_async_copy(wu_hbm.at[layer, e], wus.at[slot], sems.at[slot]),
                pltpu.make_async_copy(wd_hbm.at[layer, e], wds.at[slot], sems.at[slot]))

    @pl.when(j < nu_ref[0])
    def _():
        e = be_ref[j]
        slot = seg_ref[j] % 2
        changed = jnp.logical_or(j == 0, e != be_ref[jnp.maximum(j - 1, 0)])

        @pl.when(j == 0)
        def _():
            for cp in fetch(e, 0):
                cp.start()

        @pl.when(changed)
        def _():
            for cp in fetch(e, slot):
                cp.wait()

            @pl.when(nxt_ref[j] >= 0)
            def _():
                for cp in fetch(nxt_ref[j], 1 - slot):
                    cp.start()

            wgb[...] = wgs[slot].astype(BF16)
            wub[...] = wus[slot].astype(BF16)
            wdb[...] = wds[slot].astype(BF16)

        xa, xb = _unpack_pairs(xs_ref[...])
        xa = xa.astype(BF16)
        xb = xb.astype(BF16)
        n = xa.shape[1]
        g = (jnp.dot(xa, wgb[0:n, :], preferred_element_type=F32)
             + jnp.dot(xb, wgb[n:2 * n, :], preferred_element_type=F32))
        u = (jnp.dot(xa, wub[0:n, :], preferred_element_type=F32)
             + jnp.dot(xb, wub[n:2 * n, :], preferred_element_type=F32))
        h = (g * jax.nn.sigmoid(g)) * u
        y_ref[...] = _pack_pairs(jnp.dot(h.astype(BF16), wdb[...], preferred_element_type=F32))

    @pl.when(j >= nu_ref[0])
    def _():
        y_ref[...] = jnp.zeros_like(y_ref)


def _ffn(layer, blk_e, blk_in, n_used, seg, nxt, xs, w_gate, w_up, w_down):
    n_slots, dp = xs.shape
    d, de = w_gate.shape[-2:]
    tb = MOE_TILE
    nb = n_slots // tb
    gs = pltpu.PrefetchScalarGridSpec(
        num_scalar_prefetch=5,
        grid=(nb,),
        in_specs=[pl.BlockSpec((tb, dp), lambda j, be, bi, *_: (bi[j], 0)),
                  pl.BlockSpec(memory_space=pl.ANY),
                  pl.BlockSpec(memory_space=pl.ANY),
                  pl.BlockSpec(memory_space=pl.ANY)],
        out_specs=pl.BlockSpec((tb, dp), lambda j, *_: (j, 0)),
        scratch_shapes=[pltpu.VMEM((2, d, de), F32), pltpu.VMEM((2, d, de), F32), pltpu.VMEM((2, de, d), F32),
                        pltpu.VMEM((d, de), BF16), pltpu.VMEM((d, de), BF16), pltpu.VMEM((de, d), BF16),
                        pltpu.SemaphoreType.DMA((2,))],
    )
    return pl.pallas_call(
        functools.partial(_ffn_kernel, layer=layer),
        grid_spec=gs,
        out_shape=jax.ShapeDtypeStruct((n_slots, dp), jnp.uint32),
        compiler_params=_cparams(1),
        name="moe_ffn",
    )(blk_e, blk_in, n_used, seg, nxt, xs, w_gate, w_up, w_down)


def _combine_kernel(dest_ref, dnext_ref, wts_ref, x1_ref, gate_ref, *rest, td, final):
    if final:
        fg_ref, yb_ref, o_ref, ybuf, sems = rest
    else:
        yb_ref, o_ref, ybuf, sems = rest
    i = pl.program_id(0)
    slot = i % 2

    def gather(d_ref, s):
        _row_dmas_start(td, lambda t, k: pltpu.make_async_copy(
            yb_ref.at[pl.ds(d_ref[k * td + t], 1), :], ybuf.at[s, k, pl.ds(t, 1), :], sems.at[s]))

    @pl.when(i == 0)
    def _():
        gather(dest_ref, 0)

    @pl.when(i + 1 < pl.num_programs(0))
    def _():
        gather(dnext_ref, 1 - slot)

    _row_dmas_wait(td, lambda: pltpu.make_async_copy(yb_ref.at[pl.ds(0, 1), :],
                                                     ybuf.at[slot, 0, pl.ds(0, 1), :], sems.at[slot]))
    w = wts_ref[...]
    a0, b0 = _unpack_pairs(ybuf[slot, 0])
    a1, b1 = _unpack_pairs(ybuf[slot, 1])
    y = jnp.concatenate([a0 * w[:, 0:1] + a1 * w[:, 1:2], b0 * w[:, 0:1] + b1 * w[:, 1:2]], axis=1)
    xn = x1_ref[...] + gate_ref[0] * y
    if final:
        ms = jnp.mean(xn * xn, axis=-1, keepdims=True)
        xn = (xn * lax.rsqrt(ms + EPS)) * fg_ref[...]
    o_ref[...] = xn


def _combine(tok, layer, dest, wts_rows, x1, modf, yb, final_g):
    nt, d = x1.shape
    td = tok.tile
    final = final_g is not None
    n_steps = nt // td
    in_specs = [pl.BlockSpec((2 * td,), lambda i: (i,), memory_space=pltpu.SMEM),
                pl.BlockSpec((2 * td,), lambda i: (jnp.minimum(i + 1, n_steps - 1),), memory_space=pltpu.SMEM),
                pl.BlockSpec((td, 2), lambda i: (i, 0)),
                pl.BlockSpec((td, d), lambda i: (i, 0)),
                pl.BlockSpec((1, 1, d), tok.mod_map(layer, 5))]
    args = [dest, dest, wts_rows, x1, modf]
    if final:
        in_specs.append(pl.BlockSpec((1, d), lambda i: (0, 0)))
        args.append(final_g)
    in_specs.append(pl.BlockSpec(memory_space=pl.ANY))
    args.append(yb)
    return pl.pallas_call(
        functools.partial(_combine_kernel, td=td, final=final),
        grid=(nt // td,),
        in_specs=in_specs,
        out_specs=pl.BlockSpec((td, d), lambda i: (i, 0)),
        out_shape=jax.ShapeDtypeStruct((nt, d), F32),
        scratch_shapes=[pltpu.VMEM((2, 2, td, d // 2), yb.dtype), pltpu.SemaphoreType.DMA((2,))],
        compiler_params=_cparams(1),
        name="moe_combine",
    )(*args)


def _moe(tok_dma, layer, scores_t, h2, x1, modf, router_bias, w_gate, w_up, w_down, final_g):
    nt = h2.shape[0]
    n_e = w_gate.shape[1]
    tb = MOE_TILE
    ints, wts, cnt = _route(scores_t, router_bias.astype(F32))
    counts = cnt[:, 0].astype(jnp.int32)
    padded = ((counts + tb - 1) // tb) * tb
    eidx = jnp.arange(n_e, dtype=jnp.int32)
    p_end = jnp.sum(jnp.where(eidx[None, :] <= eidx[:, None], padded[None, :], 0), axis=1)
    p_start = (p_end - padded).astype(jnp.int32)
    nb = (2 * nt) // tb + n_e
    n_used = (p_end[-1] // tb).astype(jnp.int32)
    jb = jnp.arange(nb, dtype=jnp.int32)
    be = jnp.sum((p_end[None, :] <= (jb * tb)[:, None]).astype(jnp.int32), axis=1)
    be = jnp.minimum(be, n_e - 1)
    last = jnp.maximum(n_used - 1, 0)
    blk_e = jnp.where(jb < n_used, be, be[last])
    blk_in = jnp.minimum(jb, last)
    seg = jnp.sum(jnp.where(ints[0:2, :, None] == eidx, p_start, 0), axis=-1)
    td = tok_dma.tile
    dest = (seg + ints[2:4]).reshape(2, nt // td, td).transpose(1, 0, 2).reshape(-1)
    pad0 = p_start + counts
    pad8 = jnp.minimum(((pad0 + SUBLANES - 1) // SUBLANES) * SUBLANES, p_end)
    fill = jnp.stack([pad0, pad8 - pad0, pad8, p_end - pad8]).astype(jnp.int32)
    fill = jnp.concatenate([fill, jnp.full((4, 1), 2 * n_used, jnp.int32)], axis=1)
    first = jnp.logical_and(jb < n_used, jnp.logical_or(jb == 0, blk_e != jnp.roll(blk_e, 1)))
    seg_idx = jnp.sum(jnp.where(jb[None, :] <= jb[:, None], first[None, :], False).astype(jnp.int32), axis=1) - 1
    later = jnp.logical_and(jb[None, :] < n_used, blk_e[None, :] > blk_e[:, None])
    nxt = jnp.min(jnp.where(later, blk_e[None, :], n_e), axis=1)
    nxt = jnp.where(nxt < n_e, nxt, -1).astype(jnp.int32)
    xs = _dispatch(fill, dest, h2, nb * tb, n_e)
    yb = _ffn(layer, blk_e, blk_in, n_used.reshape(1), seg_idx.astype(jnp.int32), nxt, xs, w_gate, w_up, w_down)
    return _combine(tok_dma, layer, dest, wts.T, x1, modf, yb, final_g)


def _rope_tables(S, tile):
    n_rows = S // GRID_W
    row = jnp.repeat(jnp.arange(n_rows), GRID_W).astype(F32)
    col = jnp.tile(jnp.arange(GRID_W), n_rows).astype(F32)
    n_freq = 16
    inv = 1.0 / (ROPE_THETA ** (jnp.arange(n_freq, dtype=F32) / n_freq))
    ang = jnp.stack([row, col], axis=-1)[:, :, None] * inv
    ang = jnp.broadcast_to(ang[:, :, None, :], (S, 2, 2, n_freq)).reshape(S, 4 * n_freq)
    cos, sin = jnp.cos(ang), jnp.sin(ang)
    pick = jnp.array(list(range(n_freq)) + list(range(2 * n_freq, 3 * n_freq)))
    cos_t = jnp.concatenate([cos[:, pick].T, jnp.ones((2 * n_freq, tile), F32)], axis=1)
    sin_t = jnp.concatenate([sin[:, pick].T, jnp.zeros((2 * n_freq, tile), F32)], axis=1)
    sign = jnp.where((jnp.arange(4 * n_freq) % 32) < 16, -1.0, 1.0).astype(F32)
    sin = sin * sign
    cos = jnp.concatenate([jnp.tile(cos, (1, 2)), jnp.ones((tile, LANES), F32)], axis=0)
    sin = jnp.concatenate([jnp.tile(sin, (1, 2)), jnp.zeros((tile, LANES), F32)], axis=0)
    return (cos, sin), (cos_t, sin_t)


def kernel(x, c, ctx, c_ctx, w_mod, b_mod, norm1_g, norm2_g, rg_w_in, rg_conv_w, rg_conv_b,
           rg_gate_a_w, rg_gate_a_b, rg_gate_x_w, rg_gate_x_b, rg_lambda, rg_w_out,
           da_w_qkv, da_lambda, da_subln_g, da_w_o, router_w, router_bias,
           moe_w_gate, moe_w_up, moe_w_down, final_g):
    B, S, D = x.shape
    C = ctx.shape[1]
    depth = w_mod.shape[0]
    assert B + 1 <= COND_PAD and D % LANES == 0
    assert all(l % N_MIXERS == 0 or l == depth - 1 for l in range(depth))
    tok = _Tok(B, S, C, min(ROW_TILE, S, B * C))
    tok_dma = _Tok(B, S, C, min(DMA_TILE, S, B * C))

    cond = jnp.zeros((COND_PAD, D), F32).at[:B].set(c).at[B].set(c_ctx)
    mods = _adaln(cond, w_mod, b_mod)
    modf = mods.reshape(depth * COND_PAD * 6, 1, D)
    srcs = (x.reshape(B * S, D), ctx.reshape(B * C, D))
    rw_t = router_w.T.astype(F32)
    rw_hi = rw_t.astype(BF16)
    rw_lo = (rw_t - rw_hi.astype(F32)).astype(BF16)

    i_rg = 0
    i_da = 0
    out = None
    for layer in range(depth):
        last = layer == depth - 1
        n1 = norm1_g[layer].reshape(1, D)
        n2 = norm2_g[layer].reshape(1, D)
        if layer % N_MIXERS == 0:
            d_rnn = rg_w_in.shape[2] // 2
            w_in = rg_w_in[i_rg].astype(BF16)
            y_all, u_all = _rg_in(tok, layer, srcs, n1, modf, w_in[:, :d_rnn], w_in[:, d_rnn:])
            nblk = rg_gate_a_w.shape[2]
            wg = jnp.concatenate([rg_gate_a_w[i_rg, 0], rg_gate_x_w[i_rg, 0],
                                  rg_gate_a_w[i_rg, 1], rg_gate_x_w[i_rg, 1]], axis=-1).astype(BF16)
            bsplit = lambda v: v.reshape(nblk, 1, LANES)
            gb = jnp.concatenate([bsplit(rg_gate_a_b[i_rg, 0]), bsplit(rg_gate_x_b[i_rg, 0]),
                                  bsplit(rg_gate_a_b[i_rg, 1]), bsplit(rg_gate_x_b[i_rg, 1])], axis=-1)
            g_x, g_c = _rglru(B, S, C, u_all, y_all, rg_conv_w[i_rg], rg_conv_b[i_rg].reshape(1, d_rnn),
                              wg, gb, rg_lambda[i_rg])
            w_o = rg_w_out[i_rg].astype(BF16)
            i_rg += 1
        else:
            dh = da_lambda.shape[2]
            heads = D // (2 * dh)
            lambda_init = 0.8 - 0.6 * math.exp(-0.3 * layer)
            wqkv = da_w_qkv[i_da].astype(BF16)
            tables, tables_t = _rope_tables(S, tok.tile)
            tc = min(ATTN_KEY_TILE, S, C)
            tq = min(ATTN_Q_TILE, S, tok.tile)
            assert S % tc == 0 and C % tc == 0 and tok.tile % tc == 0 and S % tq == 0 and tok.tile % tq == 0
            xall = srcs[0] if len(srcs) == 1 else jnp.concatenate(srcs, axis=0)
            qt_all, k_all, vt_all = _qkv(tok, layer, xall, n1, modf, wqkv[:, :D].T, wqkv[:, D:2 * D],
                                         wqkv[:, 2 * D:].T, tables, tables_t, tq, tc,
                                         math.log2(math.e) * dh ** -0.5)
            g_x = _attention(B, S, C, heads, qt_all, k_all, vt_all, da_lambda[i_da],
                             da_subln_g[i_da].reshape(2 * dh, 1), lambda_init)
            g_c = None
            w_o = da_w_o[i_da].astype(BF16)
            i_da += 1
        if last:
            n_rows = B * S
            g_srcs = (g_x,)
        else:
            n_rows = B * S + B * C
            g_srcs = (g_x, g_c)
        x1, h2, scores_t = _proj_router(tok, layer, n_rows, g_srcs, w_o, srcs, n2, modf, rw_hi, rw_lo)
        res = _moe(tok_dma, layer, scores_t, h2, x1, modf, router_bias,
                   moe_w_gate, moe_w_up, moe_w_down,
                   final_g.reshape(1, D) if last else None)
        if last:
            out = res.reshape(B, S, D)
        else:
            srcs = (res,)
    return out
```

```python
import functools
import math

import jax
import jax.numpy as jnp
from jax import lax
from jax.experimental import pallas as pl
from jax.experimental.pallas import tpu as pltpu

F32 = jnp.float32
BF16 = jnp.bfloat16

EPS = 1e-6
TINY = 1e-30
GRID_W = 64
CONV_W = 4
CONV_LEFT = 2
RG_C = 8.0
ROPE_THETA = 10000.0
N_GROUPS = 4
N_MIXERS = 2

LANES = 128
SUBLANES = 8
VMEM_LIMIT = 56 * 1024 * 1024

ROW_TILE = 512
MOE_TILE = 256
DMA_TILE = 1024
DMA_UNROLL = 8
WAIT_UNROLL = 32
ROUTE_TILE = 1024
ATTN_Q_TILE = 256
ATTN_KEY_TILE = 256
ATTN_KEY_GROUP = 1
ATTN_HEADS_PER_STEP = 2
ATTN_SHIFT_SLACK = 64.0
SCAN_CHUNK = 256
SCAN_UNROLL = 8
COND_PAD = 8


def _cparams(n_axes):
    return pltpu.CompilerParams(dimension_semantics=("arbitrary",) * n_axes,
                                vmem_limit_bytes=VMEM_LIMIT)


def _adaln_kernel(cond_ref, w_ref, b_ref, o_ref):
    c = cond_ref[...]
    h = c * jax.nn.sigmoid(c)
    o_ref[0] = jnp.dot(h, w_ref[0], preferred_element_type=F32,
                       precision=lax.Precision.HIGHEST) + b_ref[0]


def _adaln(cond, w_mod, b_mod):
    depth, d, n6 = w_mod.shape
    tn = n6 // 6
    return pl.pallas_call(
        _adaln_kernel,
        grid=(depth, n6 // tn),
        in_specs=[pl.BlockSpec((COND_PAD, d), lambda l, j: (0, 0)),
                  pl.BlockSpec((1, d, tn), lambda l, j: (l, 0, j)),
                  pl.BlockSpec((1, 1, tn), lambda l, j: (l, 0, j))],
        out_specs=pl.BlockSpec((1, COND_PAD, tn), lambda l, j: (l, 0, j)),
        out_shape=jax.ShapeDtypeStruct((depth, COND_PAD, n6), F32),
        compiler_params=_cparams(2),
        name="adaln",
    )(cond, w_mod, b_mod.reshape(depth, 1, n6))


def _norm_mod(x, g, shift, scale):
    ms = jnp.mean(x * x, axis=-1, keepdims=True)
    h = (x * lax.rsqrt(ms + EPS)) * g
    return h * (1.0 + scale) + shift


class _Tok:
    def __init__(self, B, S, C, tile):
        assert S % tile == 0 and (B * C) % tile == 0
        self.B, self.S, self.C, self.tile = B, S, C, tile
        self.nx = B * S // tile
        self.nc = B * C // tile
        self.spt = S // tile

    def cond(self, i):
        return jnp.where(i < self.nx, i // self.spt, self.B)

    def mod_map(self, layer, chunk):
        return lambda i, *_: ((layer * COND_PAD + self.cond(i)) * 6 + chunk, 0, 0)


def _token_specs(tok, srcs):
    tm, d = tok.tile, srcs[0].shape[1]
    if len(srcs) == 1:
        return [pl.BlockSpec((tm, d), lambda i, *_: (i, 0))]
    return [pl.BlockSpec((tm, d), lambda i, *_: (jnp.minimum(i, tok.nx - 1), 0)),
            pl.BlockSpec((tm, d), lambda i, *_: (jnp.maximum(i - tok.nx, 0), 0))]


def _token_tile(x_refs, nx):
    if len(x_refs) == 1:
        return x_refs[0][...]
    return jnp.where(pl.program_id(0) < nx, x_refs[0][...], x_refs[1][...])


def _dot_nt(a, b):
    return lax.dot_general(a, b, (((1,), (1,)), ((), ())), preferred_element_type=F32)


def _rg_in_kernel(*refs, n_src, nx):
    x_refs, (g_ref, sh_ref, sc_ref, wy_ref, wu_ref, y_ref, u_ref) = refs[:n_src], refs[n_src:]
    h = _norm_mod(_token_tile(x_refs, nx), g_ref[...], sh_ref[0], sc_ref[0]).astype(BF16)
    y = jnp.dot(h, wy_ref[...], preferred_element_type=F32)
    y_ref[...] = jax.nn.gelu(y, approximate=True).astype(y_ref.dtype)
    u_ref[...] = jnp.dot(h, wu_ref[...], preferred_element_type=F32)


def _rg_in(tok, layer, srcs, g, modf, w_y, w_u):
    d = srcs[0].shape[1]
    tm = tok.tile
    ntok = (tok.nx + tok.nc) * tm
    nw = w_y.shape[1]
    return pl.pallas_call(
        functools.partial(_rg_in_kernel, n_src=len(srcs), nx=tok.nx),
        grid=(ntok // tm,),
        in_specs=_token_specs(tok, srcs) + [
            pl.BlockSpec((1, d), lambda i: (0, 0)),
            pl.BlockSpec((1, 1, d), tok.mod_map(layer, 0)),
            pl.BlockSpec((1, 1, d), tok.mod_map(layer, 1)),
            pl.BlockSpec((d, nw), lambda i: (0, 0)),
            pl.BlockSpec((d, nw), lambda i: (0, 0))],
        out_specs=[pl.BlockSpec((tm, nw), lambda i: (i, 0)), pl.BlockSpec((tm, nw), lambda i: (i, 0))],
        out_shape=[jax.ShapeDtypeStruct((ntok, nw), BF16), jax.ShapeDtypeStruct((ntok, nw), F32)],
        compiler_params=_cparams(1),
        name="rg_in_proj",
    )(*srcs, g, modf, modf, w_y, w_u)


def _qkv_kernel(x_ref, g_ref, sh_ref, sc_ref, wqt_ref, wk_ref, wvt_ref, cos_ref, sin_ref, cost_ref, sint_ref,
                qt_ref, k_ref, vt_ref, *, q_scale):
    h = _norm_mod(x_ref[...], g_ref[...], sh_ref[0], sc_ref[0]).astype(BF16)

    qt = _dot_nt(wqt_ref[...], h)
    cos = cost_ref[...]
    sin = sint_ref[...]
    n_q, nw, tq = qt_ref.shape
    for g in range(nw // 32):
        a = qt[g * 32:g * 32 + 16]
        b = qt[g * 32 + 16:g * 32 + 32]
        c = cos[(g % 2) * 16:(g % 2) * 16 + 16]
        s = sin[(g % 2) * 16:(g % 2) * 16 + 16]
        ra = ((a * c - b * s) * q_scale).astype(qt_ref.dtype)
        rb = ((b * c + a * s) * q_scale).astype(qt_ref.dtype)
        for ci in range(n_q):
            qt_ref[ci, g * 32:g * 32 + 16, :] = ra[:, ci * tq:(ci + 1) * tq]
            qt_ref[ci, g * 32 + 16:g * 32 + 32, :] = rb[:, ci * tq:(ci + 1) * tq]

    k = jnp.dot(h, wk_ref[...], preferred_element_type=F32)
    cos = cos_ref[...]
    sin = sin_ref[...]
    lane = lax.broadcasted_iota(jnp.int32, cos.shape, 1)
    first = (lane % 32) < 16
    for cb in range(k.shape[1] // LANES):
        a = k[:, cb * LANES:(cb + 1) * LANES]
        rot = jnp.where(first, pltpu.roll(a, LANES - 16, 1), pltpu.roll(a, 16, 1))
        k_ref[:, cb * LANES:(cb + 1) * LANES] = (a * cos + rot * sin).astype(k_ref.dtype)

    vt = _dot_nt(wvt_ref[...], h)
    n_v, _, tc = vt_ref.shape
    for ci in range(n_v):
        vt_ref[ci] = vt[:, ci * tc:(ci + 1) * tc].astype(vt_ref.dtype)


def _qkv(tok, layer, xall, g, modf, wqt, wk, wvt, tables, tables_t, tq, tc, q_scale):
    ntok, d = xall.shape
    tm = tok.tile
    tmap = lambda i: (jnp.where(i < tok.nx, i % tok.spt, tok.spt), 0)
    tmap_t = lambda i: (0, jnp.where(i < tok.nx, i % tok.spt, tok.spt))
    wspec = pl.BlockSpec((d, d), lambda i: (0, 0))
    return pl.pallas_call(
        functools.partial(_qkv_kernel, q_scale=q_scale),
        grid=(ntok // tm,),
        in_specs=[pl.BlockSpec((tm, d), lambda i: (i, 0)),
                  pl.BlockSpec((1, d), lambda i: (0, 0)),
                  pl.BlockSpec((1, 1, d), tok.mod_map(layer, 0)),
                  pl.BlockSpec((1, 1, d), tok.mod_map(layer, 1)),
                  wspec, wspec, wspec,
                  pl.BlockSpec((tm, LANES), tmap), pl.BlockSpec((tm, LANES), tmap),
                  pl.BlockSpec((32, tm), tmap_t), pl.BlockSpec((32, tm), tmap_t)],
        out_specs=[pl.BlockSpec((tm // tq, d, tq), lambda i: (i, 0, 0)),
                   pl.BlockSpec((tm, d), lambda i: (i, 0)),
                   pl.BlockSpec((tm // tc, d, tc), lambda i: (i, 0, 0))],
        out_shape=[jax.ShapeDtypeStruct((ntok // tq, d, tq), BF16),
                   jax.ShapeDtypeStruct((ntok, d), BF16),
                   jax.ShapeDtypeStruct((ntok // tc, d, tc), BF16)],
        compiler_params=_cparams(1),
        name="qkv_proj",
    )(xall, g, modf, modf, wqt, wk, wvt, *tables, *tables_t)


def _softplus(x):
    return jnp.maximum(x, 0.0) + jnp.log(1.0 + jnp.exp(-jnp.abs(x)))


def _rglru_kernel(ux_ref, uc_ref, yx_ref, yc_ref, cw_ref, cb_ref, wg_ref, gb_ref, lam_ref,
                  ox_ref, oc_ref, a_s, b_s, hf_s, hb_s, *, S, C, chunk):
    ka = _softplus(-lam_ref[...]) * (-RG_C * math.log2(math.e))
    cw = cw_ref[...]
    cb = cb_ref[...]
    wg = wg_ref[0]
    gb = gb_ref[0]

    def coeffs(u_ref, T):
        rc = min(chunk, T)

        def body(ci, _):
            c0 = pl.multiple_of(ci * rc, rc)
            cur = u_ref[pl.ds(c0, rc), :]
            pstart = pl.multiple_of(jnp.maximum(c0 - SUBLANES, 0), SUBLANES)
            nstart = pl.multiple_of(jnp.minimum(c0 + rc, T - SUBLANES), SUBLANES)
            prev = jnp.where(c0 > 0, u_ref[pl.ds(pstart, SUBLANES), :], 0.0)
            nxt = jnp.where(c0 + rc < T, u_ref[pl.ds(nstart, SUBLANES), :], 0.0)
            ext = jnp.concatenate([prev, cur, nxt], axis=0)
            n_ext = rc + 2 * SUBLANES
            xm2 = pltpu.roll(ext, 2, 0)[SUBLANES:SUBLANES + rc]
            xm1 = pltpu.roll(ext, 1, 0)[SUBLANES:SUBLANES + rc]
            xp1 = pltpu.roll(ext, n_ext - 1, 0)[SUBLANES:SUBLANES + rc]
            u = xm2 * cw[0:1] + xm1 * cw[1:2] + cur * cw[2:3] + xp1 * cw[3:4] + cb
            z = jnp.dot(u.astype(BF16), wg, preferred_element_type=F32) + gb
            for d in range(2):
                r = jax.nn.sigmoid(z[:, d * 2 * LANES:d * 2 * LANES + LANES])
                ig = jax.nn.sigmoid(z[:, d * 2 * LANES + LANES:(d + 1) * 2 * LANES])
                a = jnp.exp2(r * ka[d:d + 1])
                om = 1.0 - a * a
                root = om * lax.rsqrt(jnp.maximum(om, TINY))
                bcoef = root * (ig * u)
                a_s[d, pl.ds(c0, rc), :] = a
                b_s[d, pl.ds(c0, rc), :] = bcoef
            return 0

        lax.fori_loop(0, T // rc, body, 0)

    row = lax.broadcasted_iota(jnp.int32, (SUBLANES, LANES), 0)

    def scans(T, cf, cbk):
        def body(k, carry):
            cf, cbk = carry
            rf = pl.multiple_of(k * SUBLANES, SUBLANES)
            rb = pl.multiple_of(T - SUBLANES - k * SUBLANES, SUBLANES)
            a = a_s[0, pl.ds(rf, SUBLANES), :]
            b = b_s[0, pl.ds(rf, SUBLANES), :]
            for s in (1, 2, 4):
                m = row >= s
                b = jnp.where(m, a * pltpu.roll(b, s, 0), 0.0) + b
                a = jnp.where(m, a * pltpu.roll(a, s, 0), a)
            h = b + a * cf
            hf_s[pl.ds(rf, SUBLANES), :] = h
            cf = h[SUBLANES - 1:SUBLANES, :]
            a = a_s[1, pl.ds(rb, SUBLANES), :]
            b = b_s[1, pl.ds(rb, SUBLANES), :]
            for s in (1, 2, 4):
                m = row < SUBLANES - s
                b = jnp.where(m, a * pltpu.roll(b, SUBLANES - s, 0), 0.0) + b
                a = jnp.where(m, a * pltpu.roll(a, SUBLANES - s, 0), a)
            h = b + a * cbk
            hb_s[pl.ds(rb, SUBLANES), :] = h
            cbk = h[0:1, :]
            return cf, cbk

        return lax.fori_loop(0, T // SUBLANES, body, (cf, cbk), unroll=SCAN_UNROLL)

    zero = jnp.zeros((1, LANES), F32)
    coeffs(uc_ref, C)
    scans(C, zero, zero)
    oc_ref[...] = ((hf_s[0:C, :] + hb_s[0:C, :]) * yc_ref[...].astype(F32)).astype(oc_ref.dtype)
    h0f = hf_s[C - 1:C, :]
    h0b = hb_s[0:1, :]
    coeffs(ux_ref, S)
    scans(S, h0f, h0b)
    ox_ref[...] = ((hf_s[0:S, :] + hb_s[0:S, :]) * yx_ref[...].astype(F32)).astype(ox_ref.dtype)


def _rglru(B, S, C, u_all, y_all, conv_w, conv_b, wg, gb, lam):
    d_rnn = u_all.shape[1]
    nblk = d_rnn // LANES
    cbase = B * S // C
    tmax = max(S, C)
    return pl.pallas_call(
        functools.partial(_rglru_kernel, S=S, C=C, chunk=SCAN_CHUNK),
        grid=(B, nblk),
        in_specs=[pl.BlockSpec((S, LANES), lambda b, n: (b, n)),
                  pl.BlockSpec((C, LANES), lambda b, n: (cbase + b, n)),
                  pl.BlockSpec((S, LANES), lambda b, n: (b, n)),
                  pl.BlockSpec((C, LANES), lambda b, n: (cbase + b, n)),
                  pl.BlockSpec((CONV_W, LANES), lambda b, n: (0, n)),
                  pl.BlockSpec((1, LANES), lambda b, n: (0, n)),
                  pl.BlockSpec((1, LANES, 4 * LANES), lambda b, n: (n, 0, 0)),
                  pl.BlockSpec((1, 1, 4 * LANES), lambda b, n: (n, 0, 0)),
                  pl.BlockSpec((2, LANES), lambda b, n: (0, n))],
        out_specs=[pl.BlockSpec((S, LANES), lambda b, n: (b, n)),
                   pl.BlockSpec((C, LANES), lambda b, n: (b, n))],
        out_shape=[jax.ShapeDtypeStruct((B * S, d_rnn), BF16),
                   jax.ShapeDtypeStruct((B * C, d_rnn), BF16)],
        scratch_shapes=[pltpu.VMEM((2, tmax, LANES), F32), pltpu.VMEM((2, tmax, LANES), F32),
                        pltpu.VMEM((tmax, LANES), F32), pltpu.VMEM((tmax, LANES), F32)],
        compiler_params=_cparams(2),
        name="rglru",
    )(u_all, u_all, y_all, y_all, conv_w, conv_b, wg, gb, lam)


def _attn_kernel(q_ref, kx_ref, kc_ref, vx_ref, vc_ref, lam_ref, g_ref, o_ref, *s_refs, lambda_init):
    dh2 = g_ref.shape[0]
    n_h = q_ref.shape[1] // dh2
    tq = q_ref.shape[2]
    tk = kx_ref.shape[0] // vx_ref.shape[0]
    n_x = vx_ref.shape[0]
    group = ATTN_KEY_GROUP
    assert n_x % group == 0 and 0 < vc_ref.shape[0] <= group
    n_g = n_x // group
    gk = group * tk
    n_c = vc_ref.shape[0] * tk
    heads = range(n_h)
    hs = [slice(hh * dh2, (hh + 1) * dh2) for hh in heads]

    qz = []
    for hh in heads:
        qt = q_ref[0, hs[hh], :]
        row = lax.broadcasted_iota(jnp.int32, qt.shape, 0)
        zero = jnp.zeros_like(qt)
        qz.append(jnp.concatenate([jnp.where(row < dh2 // 2, qt, zero), jnp.where(row >= dh2 // 2, qt, zero)],
                                  axis=1))

    def scores(hh, kblk):
        return jnp.dot(kblk, qz[hh], preferred_element_type=F32)

    def xkeys(hh, g):
        return kx_ref[g * gk:(g + 1) * gk, hs[hh]]

    def xvals(hh, g):
        return jnp.concatenate([vx_ref[g * group + c, hs[hh], :] for c in range(group)], axis=1)

    def cvals(hh):
        return jnp.concatenate([vc_ref[c, hs[hh], :] for c in range(vc_ref.shape[0])], axis=1)

    lv = lam_ref[...]
    lam = (jnp.exp(jnp.sum(lv[0:1] * lv[1:2], axis=-1, keepdims=True))
           - jnp.exp(jnp.sum(lv[2:3] * lv[3:4], axis=-1, keepdims=True)) + lambda_init)

    def finish(hh, l, acc):
        ot = acc[:, :tq] / l[:, :tq] - lam * (acc[:, tq:] / l[:, tq:])
        ms = jnp.mean(ot * ot, axis=0, keepdims=True)
        ot = (ot * lax.rsqrt(ms + EPS)) * g_ref[...] * (1.0 - lambda_init)
        o_ref[:, hs[hh]] = ot.T.astype(o_ref.dtype)

    bufs = [(s_refs[2 * hh], s_refs[2 * hh + 1]) for hh in heads]
    m, l, acc, top = {}, {}, {}, {}
    sc = [scores(hh, kc_ref[:, hs[hh]]) for hh in heads]
    for hh in heads:
        bufs[hh][0][...] = scores(hh, xkeys(hh, 0))
    for hh in heads:
        m[hh] = jnp.max(sc[hh], axis=0, keepdims=True)
        p = jnp.exp2(sc[hh] - m[hh])
        l[hh] = jnp.sum(p, axis=0, keepdims=True)
        acc[hh] = jnp.dot(cvals(hh), p.astype(BF16), preferred_element_type=F32)
        top[hh] = m[hh]
    for g in range(n_g):
        for hh in heads:
            if g + 1 < n_g:
                bufs[hh][(g + 1) % 2][...] = scores(hh, xkeys(hh, g + 1))
            s = bufs[hh][g % 2][...]
            top[hh] = jnp.maximum(top[hh], jnp.max(s, axis=0, keepdims=True))
            p = jnp.exp2(s - m[hh])
            l[hh] = l[hh] + jnp.sum(p, axis=0, keepdims=True)
            acc[hh] = acc[hh] + jnp.dot(xvals(hh, g), p.astype(BF16), preferred_element_type=F32)
    for hh in heads:
        finish(hh, l[hh], acc[hh])

    def update(st, s, vt):
        mo, lo, ao = st
        mn = jnp.maximum(mo, jnp.max(s, axis=0, keepdims=True))
        alpha = jnp.exp2(mo - mn)
        p = jnp.exp2(s - mn)
        lo = alpha * lo + jnp.sum(p, axis=0, keepdims=True)
        ao = alpha * ao + jnp.dot(vt, p.astype(BF16), preferred_element_type=F32)
        return mn, lo, ao

    for hh in heads:
        excess = jnp.max(top[hh] - m[hh])

        @pl.when(jnp.logical_not(excess <= ATTN_SHIFT_SLACK))
        def _(hh=hh):
            st = (jnp.full((1, 2 * tq), -jnp.inf, F32), jnp.zeros((1, 2 * tq), F32),
                  jnp.zeros((dh2, 2 * tq), F32))
            bh = bufs[hh]
            bh[0][...] = scores(hh, xkeys(hh, 0))
            for g in range(n_g):
                nxt = bh[(g + 1) % 2]
                if g + 1 < n_g:
                    nxt[...] = scores(hh, xkeys(hh, g + 1))
                else:
                    nxt[0:n_c, :] = scores(hh, kc_ref[:, hs[hh]])
                st = update(st, bh[g % 2][...], xvals(hh, g))
            st = update(st, bh[n_g % 2][0:n_c, :], cvals(hh))
            finish(hh, st[1], st[2])


def _attention(B, S, C, heads, qt_all, k_all, vt_all, lam, subln_g, lambda_init):
    d = k_all.shape[1]
    dh2 = d // heads
    tq = qt_all.shape[2]
    tc = vt_all.shape[2]
    qpb = S // tq
    cbase = B * S // C
    hp = ATTN_HEADS_PER_STEP if heads % ATTN_HEADS_PER_STEP == 0 else 1
    hw = hp * dh2
    return pl.pallas_call(
        functools.partial(_attn_kernel, lambda_init=lambda_init),
        grid=(B, heads // hp, qpb),
        in_specs=[pl.BlockSpec((1, hw, tq), lambda b, h, i: (b * qpb + i, h, 0)),
                  pl.BlockSpec((S, hw), lambda b, h, i: (b, h)),
                  pl.BlockSpec((C, hw), lambda b, h, i: (cbase + b, h)),
                  pl.BlockSpec((S // tc, hw, tc), lambda b, h, i: (b, h, 0)),
                  pl.BlockSpec((C // tc, hw, tc), lambda b, h, i: (cbase + b, h, 0)),
                  pl.BlockSpec(lam.shape, lambda b, h, i: (0, 0)),
                  pl.BlockSpec((dh2, 1), lambda b, h, i: (0, 0))],
        out_specs=pl.BlockSpec((tq, hw), lambda b, h, i: (b * qpb + i, h)),
        out_shape=jax.ShapeDtypeStruct((B * S, d), BF16),
        scratch_shapes=[pltpu.VMEM((ATTN_KEY_GROUP * tc, 2 * tq), F32) for _ in range(2 * hp)],
        compiler_params=_cparams(3),
        name="diff_attn",
    )(qt_all, k_all, k_all, vt_all, vt_all, lam, subln_g)


def _proj_router_kernel(*refs, n_g, n_x, nx):
    g_refs, x_refs, refs = refs[:n_g], refs[n_g:n_g + n_x], refs[n_g + n_x:]
    (w_ref, gate_ref, g2_ref, sh_ref, sc_ref, rwh_ref, rwl_ref, x1_ref, h2_ref, st_ref) = refs
    o = jnp.dot(_token_tile(g_refs, nx), w_ref[...], preferred_element_type=F32)
    x1 = _token_tile(x_refs, nx) + gate_ref[0] * o
    x1_ref[...] = x1
    h2 = _norm_mod(x1, g2_ref[...], sh_ref[0], sc_ref[0])
    h2_ref[...] = _pack_pairs(h2)
    hi = h2.astype(BF16)
    lo = (h2 - hi.astype(F32)).astype(BF16)
    rwh = rwh_ref[...]
    logits = _dot_nt(rwh, hi) + _dot_nt(rwh, lo) + _dot_nt(rwl_ref[...], hi)
    st_ref[...] = jax.nn.sigmoid(logits)


def _proj_router(tok, layer, n_rows, g_srcs, w, x_srcs, norm2_g, modf, rw_hi, rw_lo):
    d = w.shape[1]
    tm = tok.tile
    e = rw_hi.shape[0]
    in_specs = _token_specs(tok, g_srcs) + _token_specs(tok, x_srcs)
    in_specs += [pl.BlockSpec((d, d), lambda i: (0, 0)),
                 pl.BlockSpec((1, 1, d), tok.mod_map(layer, 2)),
                 pl.BlockSpec((1, d), lambda i: (0, 0)),
                 pl.BlockSpec((1, 1, d), tok.mod_map(layer, 3)),
                 pl.BlockSpec((1, 1, d), tok.mod_map(layer, 4)),
                 pl.BlockSpec((e, d), lambda i: (0, 0)),
                 pl.BlockSpec((e, d), lambda i: (0, 0))]
    args = [*g_srcs, *x_srcs, w, modf, norm2_g, modf, modf, rw_hi, rw_lo]
    return pl.pallas_call(
        functools.partial(_proj_router_kernel, n_g=len(g_srcs), n_x=len(x_srcs), nx=tok.nx),
        grid=(n_rows // tm,),
        in_specs=in_specs,
        out_specs=[pl.BlockSpec((tm, d), lambda i: (i, 0)),
                   pl.BlockSpec((tm, d // 2), lambda i: (i, 0)),
                   pl.BlockSpec((e, tm), lambda i: (0, i))],
        out_shape=[jax.ShapeDtypeStruct((n_rows, d), F32),
                   jax.ShapeDtypeStruct((n_rows, d // 2), jnp.uint32),
                   jax.ShapeDtypeStruct((e, n_rows), F32)],
        compiler_params=_cparams(1),
        name="proj_router",
    )(*args)


def _route_kernel(s_ref, bias_ref, u_ref, ints_ref, wts_ref, cnt_ref, carry_ref, *, n_groups):
    i = pl.program_id(0)

    @pl.when(i == 0)
    def _():
        carry_ref[...] = jnp.zeros_like(carry_ref)

    s = s_ref[...]
    n_e, tr = s.shape
    epg = n_e // n_groups
    sel = s + bias_ref[...]
    ridx = lax.broadcasted_iota(jnp.int32, (epg, tr), 0).astype(F32)
    big = float(epg)
    best = gi = bi1 = bi2 = None
    for g in range(n_groups):
        sg = sel[g * epg:(g + 1) * epg, :]
        m1 = jnp.max(sg, axis=0, keepdims=True)
        i1 = jnp.min(jnp.where(sg == m1, ridx, big), axis=0, keepdims=True)
        sg2 = jnp.where(ridx == i1, -jnp.inf, sg)
        m2 = jnp.max(sg2, axis=0, keepdims=True)
        i2 = jnp.min(jnp.where(sg2 == m2, ridx, big), axis=0, keepdims=True)
        score = m1 + m2
        if g == 0:
            best, gi, bi1, bi2 = score, jnp.zeros_like(i1), i1, i2
        else:
            better = score > best
            best = jnp.where(better, score, best)
            gi = jnp.where(better, float(g), gi)
            bi1 = jnp.where(better, i1, bi1)
            bi2 = jnp.where(better, i2, bi2)
    e0 = gi * epg + bi1
    e1 = gi * epg + bi2
    eidx = lax.broadcasted_iota(jnp.int32, (n_e, tr), 0).astype(F32)
    oh0 = eidx == e0
    oh1 = eidx == e1
    w0 = jnp.sum(jnp.where(oh0, s, 0.0), axis=0, keepdims=True)
    w1 = jnp.sum(jnp.where(oh1, s, 0.0), axis=0, keepdims=True)
    den = w0 + w1
    wts_ref[0:1, :] = w0 / den
    wts_ref[1:2, :] = w1 / den
    f0 = jnp.where(oh0, 1.0, 0.0)
    f1 = jnp.where(oh1, 1.0, 0.0)
    u = u_ref[...]
    p0 = jnp.dot(f0.astype(BF16), u, preferred_element_type=F32)
    p1 = jnp.dot(f1.astype(BF16), u, preferred_element_type=F32)
    carry = carry_ref[...]
    c0 = jnp.sum(f0, axis=1, keepdims=True)
    c1 = jnp.sum(f1, axis=1, keepdims=True)
    rank0 = jnp.sum(f0 * (carry + p0), axis=0, keepdims=True)
    rank1 = jnp.sum(f1 * (carry + c0 + p1), axis=0, keepdims=True)
    ints_ref[0:1, :] = e0.astype(jnp.int32)
    ints_ref[1:2, :] = e1.astype(jnp.int32)
    ints_ref[2:3, :] = rank0.astype(jnp.int32)
    ints_ref[3:4, :] = rank1.astype(jnp.int32)
    carry = carry + c0 + c1
    carry_ref[...] = carry
    cnt_ref[...] = jnp.broadcast_to(carry, cnt_ref.shape)


def _route(scores_t, bias):
    n_e, nt = scores_t.shape
    tr = max(t for t in range(LANES, min(ROUTE_TILE, nt) + 1, LANES) if nt % t == 0)
    tri = jnp.triu(jnp.ones((tr, tr), BF16), k=1)
    return pl.pallas_call(
        functools.partial(_route_kernel, n_groups=N_GROUPS),
        grid=(nt // tr,),
        in_specs=[pl.BlockSpec((n_e, tr), lambda i: (0, i)),
                  pl.BlockSpec((n_e, 1), lambda i: (0, 0)),
                  pl.BlockSpec((tr, tr), lambda i: (0, 0))],
        out_specs=[pl.BlockSpec((4, tr), lambda i: (0, i)),
                   pl.BlockSpec((2, tr), lambda i: (0, i)),
                   pl.BlockSpec((n_e, LANES), lambda i: (0, 0))],
        out_shape=[jax.ShapeDtypeStruct((4, nt), jnp.int32),
                   jax.ShapeDtypeStruct((2, nt), F32),
                   jax.ShapeDtypeStruct((n_e, LANES), F32)],
        scratch_shapes=[pltpu.VMEM((n_e, 1), F32)],
        compiler_params=_cparams(1),
        name="route",
    )(scores_t, bias.reshape(n_e, 1), tri)


def _row_dmas_start(n_rows, make_copy):
    def start(tb, _):
        for u in range(DMA_UNROLL):
            t = tb * DMA_UNROLL + u
            for k in range(2):
                make_copy(t, k).start(priority=(2 * u + k) % 2)
        return 0

    lax.fori_loop(0, n_rows // DMA_UNROLL, start, 0)


def _row_dmas_wait(n_rows, sem_wait_copy):
    def wait(tb, _):
        for _u in range(2 * WAIT_UNROLL):
            sem_wait_copy().wait()
        return 0

    lax.fori_loop(0, n_rows // WAIT_UNROLL, wait, 0)


def _row_dmas(n_rows, make_copy, sem_wait_copy):
    _row_dmas_start(n_rows, make_copy)
    _row_dmas_wait(n_rows, sem_wait_copy)


def _dispatch_kernel(fill_ref, dest_ref, h_ref, xs_ref, zeros_ref, sem, fill_sem, *, td, n_e, tb):
    _row_dmas(td,
              lambda t, k: pltpu.make_async_copy(h_ref.at[pl.ds(t, 1), :],
                                                 xs_ref.at[pl.ds(dest_ref[k * td + t], 1), :], sem),
              lambda: pltpu.make_async_copy(h_ref.at[pl.ds(0, 1), :], xs_ref.at[pl.ds(0, 1), :], sem))

    @pl.when(pl.program_id(0) == 0)
    def _():
        zeros_ref[...] = jnp.zeros_like(zeros_ref)
        bits = [1 << b for b in range(tb.bit_length() - 2, 2, -1)]

        def run(phase, pred, n_rows, off):
            cp = pltpu.make_async_copy(zeros_ref.at[pl.ds(0, n_rows), :], xs_ref.at[pl.ds(off, n_rows), :],
                                       fill_sem)

            @pl.when(pred)
            def _():
                if phase == "start":
                    cp.start()
                else:
                    cp.wait()

        half = tb // 2
        n_half = xs_ref.shape[0] // half

        def tail(phase):
            def body(j, _):
                cp = pltpu.make_async_copy(zeros_ref, xs_ref.at[pl.ds(pl.multiple_of(j * half, half), half), :],
                                           fill_sem)
                if phase == "start":
                    cp.start()
                else:
                    cp.wait()
                return 0

            lax.fori_loop(fill_ref[0, n_e], n_half, body, 0)

        for phase in ("start", "wait"):
            for e in range(n_e):
                first, n_single, off, n = (fill_ref[r, e] for r in range(4))
                for r in range(SUBLANES - 1):
                    run(phase, r < n_single, 1, first + r)
                for bit in bits:
                    run(phase, (n & bit) != 0, bit, pl.multiple_of(off, SUBLANES))
                    off = off + (n & bit)
            tail(phase)


def _dispatch(td, fill, dest, h2, n_slots, n_e):
    nt, d = h2.shape
    tb = MOE_TILE
    gs = pltpu.PrefetchScalarGridSpec(
        num_scalar_prefetch=1,
        grid=(nt // td,),
        in_specs=[pl.BlockSpec((2 * td,), lambda i, f: (i,), memory_space=pltpu.SMEM),
                  pl.BlockSpec((td, d), lambda i, f: (i, 0))],
        out_specs=pl.BlockSpec(memory_space=pl.ANY),
        scratch_shapes=[pltpu.VMEM((tb // 2, d), h2.dtype), pltpu.SemaphoreType.DMA(()),
                        pltpu.SemaphoreType.DMA(())],
    )
    return pl.pallas_call(
        functools.partial(_dispatch_kernel, td=td, n_e=n_e, tb=tb),
        grid_spec=gs,
        out_shape=jax.ShapeDtypeStruct((n_slots, d), h2.dtype),
        compiler_params=_cparams(1),
        name="moe_dispatch",
    )(fill, dest, h2)


def _pack_pairs(x):
    n = x.shape[1] // 2
    hi = lax.bitcast_convert_type(x[:, :n].astype(BF16).astype(F32), jnp.uint32)
    lo = lax.bitcast_convert_type(x[:, n:].astype(BF16).astype(F32), jnp.uint32)
    return hi | (lo >> 16)


def _unpack_pairs(w):
    a = lax.bitcast_convert_type(w & jnp.uint32(0xFFFF0000), F32)
    b = lax.bitcast_convert_type(w << 16, F32)
    return a, b


def _ffn_kernel(be_ref, bi_ref, nu_ref, seg_ref, nxt_ref, xs_ref, wg_hbm, wu_hbm, wd_hbm, y_ref,
                wgs, wus, wds, wgb, wub, wdb, sems, *, layer):
    j = pl.program_id(0)

    def fetch(e, slot):
        return (pltpu.make_async_copy(wg_hbm.at[layer, e], wgs.at[slot], sems.at[slot]),
                pltpu.make_async_copy(wu_hbm.at[layer, e], wus.at[slot], sems.at[slot]),
                pltpu.make_async_copy(wd_hbm.at[layer, e], wds.at[slot], sems.at[slot]))

    @pl.when(j < nu_ref[0])
    def _():
        e = be_ref[j]
        slot = seg_ref[j] % 2
        changed = jnp.logical_or(j == 0, e != be_ref[jnp.maximum(j - 1, 0)])

        @pl.when(j == 0)
        def _():
            for cp in fetch(e, 0):
                cp.start()

        @pl.when(changed)
        def _():
            for cp in fetch(e, slot):
                cp.wait()

            @pl.when(nxt_ref[j] >= 0)
            def _():
                for cp in fetch(nxt_ref[j], 1 - slot):
                    cp.start()

            wgb[...] = wgs[slot].astype(BF16)
            wub[...] = wus[slot].astype(BF16)
            wdb[...] = wds[slot].astype(BF16)

        xa, xb = _unpack_pairs(xs_ref[...])
        xa = xa.astype(BF16)
        xb = xb.astype(BF16)
        n = xa.shape[1]
        g = (jnp.dot(xa, wgb[0:n, :], preferred_element_type=F32)
             + jnp.dot(xb, wgb[n:2 * n, :], preferred_element_type=F32))
        u = (jnp.dot(xa, wub[0:n, :], preferred_element_type=F32)
             + jnp.dot(xb, wub[n:2 * n, :], preferred_element_type=F32))
        h = (g * jax.nn.sigmoid(g)) * u
        y_ref[...] = _pack_pairs(jnp.dot(h.astype(BF16), wdb[...], preferred_element_type=F32))

    @pl.when(j >= nu_ref[0])
    def _():
        y_ref[...] = jnp.zeros_like(y_ref)


def _ffn(layer, blk_e, blk_in, n_used, seg, nxt, xs, w_gate, w_up, w_down):
    n_slots, dp = xs.shape
    d, de = w_gate.shape[-2:]
    tb = MOE_TILE
    nb = n_slots // tb
    gs = pltpu.PrefetchScalarGridSpec(
        num_scalar_prefetch=5,
        grid=(nb,),
        in_specs=[pl.BlockSpec((tb, dp), lambda j, be, bi, *_: (bi[j], 0)),
                  pl.BlockSpec(memory_space=pl.ANY),
                  pl.BlockSpec(memory_space=pl.ANY),
                  pl.BlockSpec(memory_space=pl.ANY)],
        out_specs=pl.BlockSpec((tb, dp), lambda j, *_: (j, 0)),
        scratch_shapes=[pltpu.VMEM((2, d, de), F32), pltpu.VMEM((2, d, de), F32), pltpu.VMEM((2, de, d), F32),
                        pltpu.VMEM((d, de), BF16), pltpu.VMEM((d, de), BF16), pltpu.VMEM((de, d), BF16),
                        pltpu.SemaphoreType.DMA((2,))],
    )
    return pl.pallas_call(
        functools.partial(_ffn_kernel, layer=layer),
        grid_spec=gs,
        out_shape=jax.ShapeDtypeStruct((n_slots, dp), jnp.uint32),
        compiler_params=_cparams(1),
        name="moe_ffn",
    )(blk_e, blk_in, n_used, seg, nxt, xs, w_gate, w_up, w_down)


def _combine_kernel(dest_ref, dnext_ref, wts_ref, x1_ref, gate_ref, *rest, td, final):
    if final:
        fg_ref, yb_ref, o_ref, ybuf, sems = rest
    else:
        yb_ref, o_ref, ybuf, sems = rest
    i = pl.program_id(0)
    slot = i % 2

    def gather(d_ref, s):
        _row_dmas_start(td, lambda t, k: pltpu.make_async_copy(
            yb_ref.at[pl.ds(d_ref[k * td + t], 1), :], ybuf.at[s, k, pl.ds(t, 1), :], sems.at[s]))

    @pl.when(i == 0)
    def _():
        gather(dest_ref, 0)

    def step(s):
        @pl.when(i + 1 < pl.num_programs(0))
        def _():
            gather(dnext_ref, 1 - s)

        _row_dmas_wait(td, lambda: pltpu.make_async_copy(yb_ref.at[pl.ds(0, 1), :],
                                                         ybuf.at[s, 0, pl.ds(0, 1), :], sems.at[s]))
        w = wts_ref[...]
        a0, b0 = _unpack_pairs(ybuf[s, 0])
        a1, b1 = _unpack_pairs(ybuf[s, 1])
        y = jnp.concatenate([a0 * w[:, 0:1] + a1 * w[:, 1:2], b0 * w[:, 0:1] + b1 * w[:, 1:2]], axis=1)
        xn = x1_ref[...] + gate_ref[0] * y
        if final:
            ms = jnp.mean(xn * xn, axis=-1, keepdims=True)
            xn = (xn * lax.rsqrt(ms + EPS)) * fg_ref[...]
        o_ref[...] = xn

    for s in range(2):
        pl.when(slot == s)(functools.partial(step, s))


def _combine(tok, layer, dest, wts_rows, x1, modf, yb, final_g):
    nt, d = x1.shape
    td = tok.tile
    final = final_g is not None
    n_steps = nt // td
    in_specs = [pl.BlockSpec((2 * td,), lambda i: (i,), memory_space=pltpu.SMEM),
                pl.BlockSpec((2 * td,), lambda i: (jnp.minimum(i + 1, n_steps - 1),), memory_space=pltpu.SMEM),
                pl.BlockSpec((td, 2), lambda i: (i, 0)),
                pl.BlockSpec((td, d), lambda i: (i, 0)),
                pl.BlockSpec((1, 1, d), tok.mod_map(layer, 5))]
    args = [dest, dest, wts_rows, x1, modf]
    if final:
        in_specs.append(pl.BlockSpec((1, d), lambda i: (0, 0)))
        args.append(final_g)
    in_specs.append(pl.BlockSpec(memory_space=pl.ANY))
    args.append(yb)
    return pl.pallas_call(
        functools.partial(_combine_kernel, td=td, final=final),
        grid=(nt // td,),
        in_specs=in_specs,
        out_specs=pl.BlockSpec((td, d), lambda i: (i, 0)),
        out_shape=jax.ShapeDtypeStruct((nt, d), F32),
        scratch_shapes=[pltpu.VMEM((2, 2, td, d // 2), yb.dtype), pltpu.SemaphoreType.DMA((2,))],
        compiler_params=_cparams(1),
        name="moe_combine",
    )(*args)


def _moe(tok_dma, layer, scores_t, h2, x1, modf, router_bias, w_gate, w_up, w_down, final_g):
    nt = h2.shape[0]
    n_e = w_gate.shape[1]
    tb = MOE_TILE
    ints, wts, cnt = _route(scores_t, router_bias.astype(F32))
    counts = cnt[:, 0].astype(jnp.int32)
    padded = ((counts + tb - 1) // tb) * tb
    eidx = jnp.arange(n_e, dtype=jnp.int32)
    p_end = jnp.sum(jnp.where(eidx[None, :] <= eidx[:, None], padded[None, :], 0), axis=1)
    p_start = (p_end - padded).astype(jnp.int32)
    nb = (2 * nt) // tb + n_e
    n_used = (p_end[-1] // tb).astype(jnp.int32)
    jb = jnp.arange(nb, dtype=jnp.int32)
    be = jnp.sum((p_end[None, :] <= (jb * tb)[:, None]).astype(jnp.int32), axis=1)
    be = jnp.minimum(be, n_e - 1)
    last = jnp.maximum(n_used - 1, 0)
    blk_e = jnp.where(jb < n_used, be, be[last])
    blk_in = jnp.minimum(jb, last)
    seg = jnp.sum(jnp.where(ints[0:2, :, None] == eidx, p_start, 0), axis=-1)
    td = tok_dma.tile
    dest = (seg + ints[2:4]).reshape(2, nt // td, td).transpose(1, 0, 2).reshape(-1)
    pad0 = p_start + counts
    pad8 = jnp.minimum(((pad0 + SUBLANES - 1) // SUBLANES) * SUBLANES, p_end)
    fill = jnp.stack([pad0, pad8 - pad0, pad8, p_end - pad8]).astype(jnp.int32)
    fill = jnp.concatenate([fill, jnp.full((4, 1), 2 * n_used, jnp.int32)], axis=1)
    first = jnp.logical_and(jb < n_used, jnp.logical_or(jb == 0, blk_e != jnp.roll(blk_e, 1)))
    seg_idx = jnp.sum(jnp.where(jb[None, :] <= jb[:, None], first[None, :], False).astype(jnp.int32), axis=1) - 1
    later = jnp.logical_and(jb[None, :] < n_used, blk_e[None, :] > blk_e[:, None])
    nxt = jnp.min(jnp.where(later, blk_e[None, :], n_e), axis=1)
    nxt = jnp.where(nxt < n_e, nxt, -1).astype(jnp.int32)
    xs = _dispatch(td, fill, dest, h2, nb * tb, n_e)
    yb = _ffn(layer, blk_e, blk_in, n_used.reshape(1), seg_idx.astype(jnp.int32), nxt, xs, w_gate, w_up, w_down)
    return _combine(tok_dma, layer, dest, wts.T, x1, modf, yb, final_g)


def _rope_tables(S, tile):
    n_rows = S // GRID_W
    row = jnp.repeat(jnp.arange(n_rows), GRID_W).astype(F32)
    col = jnp.tile(jnp.arange(GRID_W), n_rows).astype(F32)
    n_freq = 16
    inv = 1.0 / (ROPE_THETA ** (jnp.arange(n_freq, dtype=F32) / n_freq))
    ang = jnp.stack([row, col], axis=-1)[:, :, None] * inv
    ang = jnp.broadcast_to(ang[:, :, None, :], (S, 2, 2, n_freq)).reshape(S, 4 * n_freq)
    cos, sin = jnp.cos(ang), jnp.sin(ang)
    pick = jnp.array(list(range(n_freq)) + list(range(2 * n_freq, 3 * n_freq)))
    cos_t = jnp.concatenate([cos[:, pick].T, jnp.ones((2 * n_freq, tile), F32)], axis=1)
    sin_t = jnp.concatenate([sin[:, pick].T, jnp.zeros((2 * n_freq, tile), F32)], axis=1)
    sign = jnp.where((jnp.arange(4 * n_freq) % 32) < 16, -1.0, 1.0).astype(F32)
    sin = sin * sign
    cos = jnp.concatenate([jnp.tile(cos, (1, 2)), jnp.ones((tile, LANES), F32)], axis=0)
    sin = jnp.concatenate([jnp.tile(sin, (1, 2)), jnp.zeros((tile, LANES), F32)], axis=0)
    return (cos, sin), (cos_t, sin_t)


def kernel(x, c, ctx, c_ctx, w_mod, b_mod, norm1_g, norm2_g, rg_w_in, rg_conv_w, rg_conv_b,
           rg_gate_a_w, rg_gate_a_b, rg_gate_x_w, rg_gate_x_b, rg_lambda, rg_w_out,
           da_w_qkv, da_lambda, da_subln_g, da_w_o, router_w, router_bias,
           moe_w_gate, moe_w_up, moe_w_down, final_g):
    B, S, D = x.shape
    C = ctx.shape[1]
    depth = w_mod.shape[0]
    assert B + 1 <= COND_PAD and D % LANES == 0
    assert all(l % N_MIXERS == 0 or l == depth - 1 for l in range(depth))
    tok = _Tok(B, S, C, min(ROW_TILE, S, B * C))
    tok_dma = _Tok(B, S, C, min(DMA_TILE, S, B * C))

    cond = jnp.zeros((COND_PAD, D), F32).at[:B].set(c).at[B].set(c_ctx)
    mods = _adaln(cond, w_mod, b_mod)
    modf = mods.reshape(depth * COND_PAD * 6, 1, D)
    srcs = (x.reshape(B * S, D), ctx.reshape(B * C, D))
    rw_t = router_w.T.astype(F32)
    rw_hi = rw_t.astype(BF16)
    rw_lo = (rw_t - rw_hi.astype(F32)).astype(BF16)

    i_rg = 0
    i_da = 0
    out = None
    for layer in range(depth):
        last = layer == depth - 1
        n1 = norm1_g[layer].reshape(1, D)
        n2 = norm2_g[layer].reshape(1, D)
        if layer % N_MIXERS == 0:
            d_rnn = rg_w_in.shape[2] // 2
            w_in = rg_w_in[i_rg].astype(BF16)
            y_all, u_all = _rg_in(tok, layer, srcs, n1, modf, w_in[:, :d_rnn], w_in[:, d_rnn:])
            nblk = rg_gate_a_w.shape[2]
            wg = jnp.concatenate([rg_gate_a_w[i_rg, 0], rg_gate_x_w[i_rg, 0],
                                  rg_gate_a_w[i_rg, 1], rg_gate_x_w[i_rg, 1]], axis=-1).astype(BF16)
            bsplit = lambda v: v.reshape(nblk, 1, LANES)
            gb = jnp.concatenate([bsplit(rg_gate_a_b[i_rg, 0]), bsplit(rg_gate_x_b[i_rg, 0]),
                                  bsplit(rg_gate_a_b[i_rg, 1]), bsplit(rg_gate_x_b[i_rg, 1])], axis=-1)
            g_x, g_c = _rglru(B, S, C, u_all, y_all, rg_conv_w[i_rg], rg_conv_b[i_rg].reshape(1, d_rnn),
                              wg, gb, rg_lambda[i_rg])
            w_o = rg_w_out[i_rg].astype(BF16)
            i_rg += 1
        else:
            dh = da_lambda.shape[2]
            heads = D // (2 * dh)
            lambda_init = 0.8 - 0.6 * math.exp(-0.3 * layer)
            wqkv = da_w_qkv[i_da].astype(BF16)
            tables, tables_t = _rope_tables(S, tok.tile)
            tc = min(ATTN_KEY_TILE, S, C)
            tq = min(ATTN_Q_TILE, S, tok.tile)
            assert S % tc == 0 and C % tc == 0 and tok.tile % tc == 0 and S % tq == 0 and tok.tile % tq == 0
            xall = srcs[0] if len(srcs) == 1 else jnp.concatenate(srcs, axis=0)
            qt_all, k_all, vt_all = _qkv(tok, layer, xall, n1, modf, wqkv[:, :D].T, wqkv[:, D:2 * D],
                                         wqkv[:, 2 * D:].T, tables, tables_t, tq, tc,
                                         math.log2(math.e) * dh ** -0.5)
            g_x = _attention(B, S, C, heads, qt_all, k_all, vt_all, da_lambda[i_da],
                             da_subln_g[i_da].reshape(2 * dh, 1), lambda_init)
            g_c = None
            w_o = da_w_o[i_da].astype(BF16)
            i_da += 1
        if last:
            n_rows = B * S
            g_srcs = (g_x,)
        else:
            n_rows = B * S + B * C
            g_srcs = (g_x, g_c)
        x1, h2, scores_t = _proj_router(tok, layer, n_rows, g_srcs, w_o, srcs, n2, modf, rw_hi, rw_lo)
        res = _moe(tok_dma, layer, scores_t, h2, x1, modf, router_bias,
                   moe_w_gate, moe_w_up, moe_w_down,
                   final_g.reshape(1, D) if last else None)
        if last:
            out = res.reshape(B, S, D)
        else:
            srcs = (res,)
    return out
```
